```python
import math
import jax, jax.numpy as jnp
from jax import lax
import numpy as np

D_MODEL = 1024
BATCH = 8
SEQ = 2048
DEPTH = 2
DEC_BATCH = 32
DEC_SEQ = 8
PAST_LEN = 8192
PAGE_SIZE = 128

N_MEM = 256
MEM_HEADS = 4
MEM_HEAD_DIM = D_MODEL // MEM_HEADS
MIX_WIDTH = D_MODEL
GDN_HEADS = 4
GDN_HEAD_DIM = MIX_WIDTH // (2 * GDN_HEADS)
GDN_WIDTH = GDN_HEADS * GDN_HEAD_DIM
GDN_CONV_CH = 3 * GDN_WIDTH
CONV_WIDTH = 4
GDN_CHUNK = 64
GDN_IN = 4 * GDN_WIDTH + 2 * GDN_HEADS
RWKV_HEAD_DIM = 64
RWKV_WIDTH = MIX_WIDTH - GDN_WIDTH
RWKV_HEADS = RWKV_WIDTH // RWKV_HEAD_DIM
DECAY_LORA = 64
ICLR_LORA = 64
GATE_LORA = 128
RWKV_IN = 3 * RWKV_WIDTH + DECAY_LORA + ICLR_LORA + GATE_LORA
EVEN_IN = GDN_IN + RWKV_IN
SB_HEADS = 16
SB_HEAD_DIM = MIX_WIDTH // SB_HEADS
Q_BLOCK = 128
SB_BIAS_INIT = -8.0
D_FF = ((8 * D_MODEL + 3 * 256 - 1) // (3 * 256)) * 256
N_EVEN = (DEPTH + 1) // 2
N_ODD = DEPTH // 2
EPS = 1e-6
RWKV_GN_EPS = 64e-5

kernel_name = 'hybrid_gdn_rwkv7_stickbreak_step'


def rmsnorm(x, g):
    xf = x.astype(jnp.float32)
    y = xf * lax.rsqrt(jnp.mean(xf * xf, axis=-1, keepdims=True) + EPS)
    return (y * g.astype(jnp.float32)).astype(x.dtype)


def l2norm(x):
    xf = x.astype(jnp.float32)
    return xf * lax.rsqrt(jnp.sum(xf * xf, axis=-1, keepdims=True) + EPS)


def causal_conv(u, buf, w):
    t = u.shape[1]
    ext = jnp.concatenate([buf.astype(u.dtype), u], axis=1)
    out = ext[:, 0:t] * w[0]
    for j in range(1, CONV_WIDTH):
        out = out + ext[:, j:j + t] * w[j]
    return out, ext[:, t:]


def gated_delta_chunked(q, k, v, beta, g, s0):
    b, t, h, dk = k.shape
    dv = v.shape[-1]
    c = GDN_CHUNK
    n = -(-t // c)
    pad = n * c - t

    def to_chunks(a):
        a = jnp.pad(a, [(0, 0), (0, pad)] + [(0, 0)] * (a.ndim - 2))
        a = a.reshape((b, n, c) + a.shape[2:])
        return jnp.swapaxes(jnp.moveaxis(a, 1, 0), 2, 3)

    qc, kc, vc, bc, gc = (to_chunks(a) for a in (q, k, v, beta, g))
    incl = jnp.tril(jnp.ones((c, c), dtype=bool))
    strict = jnp.tril(jnp.ones((c, c), dtype=bool), -1)
    eye = jnp.eye(c, dtype=jnp.float32)

    def chunk_step(s, inp):
        qb, kb, vb, bb, gb = inp
        gcum = jnp.cumsum(gb, axis=-1)
        gam = jnp.exp(gcum)
        decay = jnp.exp(jnp.where(incl, gcum[..., :, None] - gcum[..., None, :], -jnp.inf))
        kkt = jnp.einsum('bhik,bhjk->bhij', kb, kb)
        lmat = jnp.where(strict, bb[..., :, None] * kkt * decay, 0.0)
        rhs = bb[..., None] * (vb - gam[..., None] * jnp.einsum('bhck,bhkv->bhcv', kb, s))
        u = lax.linalg.triangular_solve(lmat + eye, rhs, left_side=True, lower=True, unit_diagonal=True)
        qkt = jnp.einsum('bhik,bhjk->bhij', qb, kb) * decay
        o = gam[..., None] * jnp.einsum('bhck,bhkv->bhcv', qb, s) + jnp.einsum('bhij,bhjv->bhiv', qkt, u)
        glast = gcum[..., -1:]
        s_new = jnp.exp(glast)[..., None] * s + jnp.einsum('bhck,bhcv->bhkv', kb * jnp.exp(glast - gcum)[..., None], u)
        return s_new, o

    s_fin, oc = lax.scan(chunk_step, s0, (qc, kc, vc, bc, gc))
    o = jnp.moveaxis(jnp.swapaxes(oc, 2, 3), 0, 1).reshape(b, n * c, h, dv)[:, :t]
    return o, s_fin


def rwkv7_scan(r, w, k, v, kk, a, s0):
    def step(s, inp):
        rt, wt, kt, vt, kkt, at = inp
        sa = jnp.einsum('bhvk,bhk->bhv', s, -kkt)
        s = s * wt[:, :, None, :] + sa[..., None] * (kkt * at)[:, :, None, :] + vt[..., None] * kt[:, :, None, :]
        return s, jnp.einsum('bhvk,bhk->bhv', s, rt)

    xs = tuple(jnp.moveaxis(u, 1, 0) for u in (r, w, k, v, kk, a))
    s_fin, ys = lax.scan(step, s0, xs)
    return jnp.moveaxis(ys, 0, 1), s_fin


def even_mixer(h, w_in, w_out, conv_w, a_log, dt_bias, gdn_norm, mu, w0, w2, a0, a2, g2, k_k, k_a, r_k, gn_g, gn_b,
               conv_buf, s_gdn, s_rwkv, shift_buf):
    b, t, _ = h.shape
    f32 = jnp.float32
    hn = (RWKV_HEADS, RWKV_HEAD_DIM)
    proj = h @ w_in
    gp, rp = proj[..., :GDN_IN], proj[..., GDN_IN:]
    qkv, conv_new = causal_conv(gp[..., :GDN_CONV_CH], conv_buf, conv_w)
    qkv = jax.nn.silu(qkv).reshape(b, t, 3, GDN_HEADS, GDN_HEAD_DIM)
    q = l2norm(qkv[:, :, 0]) * (GDN_HEAD_DIM ** -0.5)
    k = l2norm(qkv[:, :, 1])
    v = qkv[:, :, 2].astype(f32)
    z = gp[..., GDN_CONV_CH:4 * GDN_WIDTH].reshape(b, t, GDN_HEADS, GDN_HEAD_DIM).astype(f32)
    beta = jax.nn.sigmoid(gp[..., 4 * GDN_WIDTH:4 * GDN_WIDTH + GDN_HEADS].astype(f32))
    g = -jnp.exp(a_log.astype(f32)) * jax.nn.softplus(gp[..., 4 * GDN_WIDTH + GDN_HEADS:].astype(f32) + dt_bias.astype(f32))
    o_a, s_gdn_new = gated_delta_chunked(q, k, v, beta, g, s_gdn.astype(f32))
    o_a = (rmsnorm(o_a, gdn_norm) * jax.nn.silu(z)).reshape(b, t, GDN_WIDTH).astype(h.dtype)
    prev = jnp.concatenate([shift_buf[:, None].astype(rp.dtype), rp[:, :-1]], axis=1)
    xr = rp + (prev - rp) * mu
    shift_new = rp[:, -1]
    wd = RWKV_WIDTH
    off = 3 * wd
    r = xr[..., :wd]
    kr = xr[..., wd:2 * wd]
    vr = xr[..., 2 * wd:3 * wd]
    pw = xr[..., off:off + DECAY_LORA]
    pa = xr[..., off + DECAY_LORA:off + DECAY_LORA + ICLR_LORA]
    pg = xr[..., off + DECAY_LORA + ICLR_LORA:]
    w_raw = (w0 + jnp.tanh(pw) @ w2).astype(f32)
    decay = jnp.exp(-jnp.exp(-jax.nn.softplus(-w_raw) - 0.5))
    a = jax.nn.sigmoid((a0 + pa @ a2).astype(f32))
    gate = (jax.nn.sigmoid(pg) @ g2).astype(f32)
    r, kr, vr, decay, a = (u.astype(f32).reshape(b, t, RWKV_HEADS, RWKV_HEAD_DIM) for u in (r, kr, vr, decay, a))
    kk = l2norm(kr * k_k.astype(f32).reshape(hn))
    kr = kr * (1.0 + (a - 1.0) * k_a.astype(f32).reshape(hn))
    y, s_rwkv_new = rwkv7_scan(r, decay, kr, vr, kk, a, s_rwkv.astype(f32))
    mean = jnp.mean(y, axis=-1, keepdims=True)
    var = jnp.mean(jnp.square(y - mean), axis=-1, keepdims=True)
    y = (y - mean) * lax.rsqrt(var + RWKV_GN_EPS) * gn_g.astype(f32).reshape(hn) + gn_b.astype(f32).reshape(hn)
    y = y + jnp.sum(r * kr * r_k.astype(f32).reshape(hn), axis=-1, keepdims=True) * vr
    o_b = (y.reshape(b, t, wd) * gate).astype(h.dtype)
    out = jnp.concatenate([o_a, o_b], axis=-1) @ w_out
    return out, conv_new, s_gdn_new, s_rwkv_new, shift_new


def sb_block(qb, k, v, q_pos, bias):
    k_pos = jnp.arange(k.shape[1])
    z = jnp.einsum('bqhd,bkhd->bhqk', qb, k).astype(jnp.float32) * (SB_HEAD_DIM ** -0.5)
    z = z + bias.astype(jnp.float32)[None, :, None, None]
    mask = k_pos[None, :] < q_pos[:, None]
    log_rem = jnp.where(mask, jax.nn.log_sigmoid(-z), 0.0)
    rev = lax.cumsum(log_rem, axis=3, reverse=True)
    suffix = jnp.concatenate([rev[..., 1:], jnp.zeros_like(rev[..., :1])], axis=-1)
    att = jnp.where(mask, jnp.exp(jax.nn.log_sigmoid(z) + suffix), 0.0)
    return jnp.einsum('bhqk,bkhd->bqhd', att.astype(v.dtype), v)


def stick_breaking(q, k, v, q_start, bias):
    t = q.shape[1]
    blk = Q_BLOCK if t % Q_BLOCK == 0 else t
    outs = []
    for i in range(t // blk):
        s = i * blk
        kend = q_start + s + blk
        outs.append(sb_block(q[:, s:s + blk], k[:, :kend], v[:, :kend], q_start + s + jnp.arange(blk), bias))
    return jnp.concatenate(outs, axis=1)


def odd_mixer(h, w_in, w_out, bias, k_past, v_past):
    b, t, _ = h.shape
    qkv = (h @ w_in).reshape(b, t, 3, SB_HEADS, SB_HEAD_DIM)
    q, k, v = qkv[:, :, 0], qkv[:, :, 1], qkv[:, :, 2]
    q_start = k_past.shape[1]
    k_all = jnp.concatenate([k_past.astype(k.dtype), k], axis=1)
    v_all = jnp.concatenate([v_past.astype(v.dtype), v], axis=1)
    y = stick_breaking(q, k_all, v_all, q_start, bias)
    return y.reshape(b, t, MIX_WIDTH) @ w_out, k, v


def mem_kv(mem, g, w_k, w_v):
    b, m, _ = mem.shape
    mn = rmsnorm(mem, g)
    return ((mn @ w_k).reshape(b, m, MEM_HEADS, MEM_HEAD_DIM), (mn @ w_v).reshape(b, m, MEM_HEADS, MEM_HEAD_DIM))


def mem_attend(h, w_q, w_o, mk, mv):
    b, t, _ = h.shape
    q = (h @ w_q).reshape(b, t, MEM_HEADS, MEM_HEAD_DIM)
    s = jnp.einsum('bqhd,bkhd->bhqk', q, mk.astype(q.dtype)).astype(jnp.float32) * (MEM_HEAD_DIM ** -0.5)
    p = jax.nn.softmax(s, axis=-1)
    o = jnp.einsum('bhqk,bkhd->bqhd', p.astype(h.dtype), mv.astype(h.dtype))
    return o.reshape(b, t, D_MODEL) @ w_o


def swiglu(h, w_in, w_out):
    gu = h @ w_in
    return (jax.nn.silu(gu[..., :D_FF]) * gu[..., D_FF:]) @ w_out


def gather_pages(pool, page_table):
    rows = pool[page_table]
    return rows.reshape((page_table.shape[0], -1) + pool.shape[2:])


def run_trunk(x, mem_k, mem_v, conv_buf, s_gdn, s_rwkv, shift, sb_past, weights):
    (norm_mix, norm_mem, norm_ffn, norm_final, ev_w_in, ev_w_out, gdn_conv_w, gdn_a_log, gdn_dt_bias, gdn_norm,
     rwkv_mu, rwkv_w0, rwkv_w2, rwkv_a0, rwkv_a2, rwkv_g2, rwkv_k_k, rwkv_k_a, rwkv_r_k, rwkv_gn_g, rwkv_gn_b,
     sb_w_in, sb_w_out, sb_bias, mem_w_q, mem_w_o, ffn_w_in, ffn_w_out) = weights
    conv_new, gdn_new, rwkv_new, shift_new, k_new, v_new = [], [], [], [], [], []
    for layer in range(DEPTH):
        i = layer // 2
        h = rmsnorm(x, norm_mix[layer])
        if layer % 2 == 0:
            mix, cb, sg, sr, sh = even_mixer(
                h, ev_w_in[i], ev_w_out[i], gdn_conv_w[i], gdn_a_log[i], gdn_dt_bias[i], gdn_norm[i],
                rwkv_mu[i], rwkv_w0[i], rwkv_w2[i], rwkv_a0[i], rwkv_a2[i], rwkv_g2[i], rwkv_k_k[i], rwkv_k_a[i],
                rwkv_r_k[i], rwkv_gn_g[i], rwkv_gn_b[i], conv_buf[i], s_gdn[i], s_rwkv[i], shift[i])
            conv_new.append(cb)
            gdn_new.append(sg)
            rwkv_new.append(sr)
            shift_new.append(sh)
        else:
            k_past, v_past = sb_past(i)
            mix, kn, vn = odd_mixer(h, sb_w_in[i], sb_w_out[i], sb_bias[i], k_past, v_past)
            k_new.append(kn)
            v_new.append(vn)
        x = x + mix
        x = x + mem_attend(rmsnorm(x, norm_mem[layer]), mem_w_q[layer], mem_w_o[layer], mem_k[layer], mem_v[layer])
        x = x + swiglu(rmsnorm(x, norm_ffn[layer]), ffn_w_in[layer], ffn_w_out[layer])
    return (rmsnorm(x, norm_final), jnp.stack(conv_new), jnp.stack(gdn_new), jnp.stack(rwkv_new),
            jnp.stack(shift_new), jnp.stack(k_new), jnp.stack(v_new))


def setup_inputs(seed: int = 0) -> dict:
    key = jax.random.key(seed)
    ks = iter(jax.random.split(key, 64))
    f32 = jnp.float32

    def nrm(shape, scale):
        return jax.random.normal(next(ks), shape, f32) * scale

    def unif(shape, lo, hi):
        return jax.random.uniform(next(ks), shape, f32, lo, hi)

    def gain(shape):
        return 1.0 + nrm(shape, 0.01)

    n_pages = PAST_LEN // PAGE_SIZE
    n_used = DEC_BATCH * n_pages
    n_pool = n_used + (n_used + 3) // 4
    page_table = jax.random.permutation(next(ks), n_pool)[:n_used].reshape(DEC_BATCH, n_pages).astype(jnp.int32)
    dt = jnp.exp(unif((N_EVEN, GDN_HEADS), math.log(1e-3), math.log(1e-1)))
    return {
        'x_prompt': nrm((BATCH, SEQ, D_MODEL), 1.0),
        'x_sample': nrm((DEC_BATCH, DEC_SEQ, D_MODEL), 1.0),
        'mem_prompt': nrm((BATCH, N_MEM, D_MODEL), 1.0),
        'state_gdn': nrm((N_EVEN, DEC_BATCH, GDN_HEADS, GDN_HEAD_DIM, GDN_HEAD_DIM), 0.1),
        'state_gdn_conv': nrm((N_EVEN, DEC_BATCH, CONV_WIDTH - 1, GDN_CONV_CH), 1.0),
        'state_rwkv': nrm((N_EVEN, DEC_BATCH, RWKV_HEADS, RWKV_HEAD_DIM, RWKV_HEAD_DIM), 0.3),
        'state_rwkv_shift': nrm((N_EVEN, DEC_BATCH, RWKV_IN), 1.0),
        'cache_sb_k': nrm((N_ODD, n_pool, PAGE_SIZE, SB_HEADS, SB_HEAD_DIM), 1.0),
        'cache_sb_v': nrm((N_ODD, n_pool, PAGE_SIZE, SB_HEADS, SB_HEAD_DIM), 1.0),
        'cache_mem_k': nrm((DEPTH, DEC_BATCH, N_MEM, MEM_HEADS, MEM_HEAD_DIM), 1.0),
        'cache_mem_v': nrm((DEPTH, DEC_BATCH, N_MEM, MEM_HEADS, MEM_HEAD_DIM), 1.0),
        'page_table': page_table,
        'norm_mix': gain((DEPTH, D_MODEL)),
        'norm_mem': gain((DEPTH, D_MODEL)),
        'norm_memtok': gain((DEPTH, D_MODEL)),
        'norm_ffn': gain((DEPTH, D_MODEL)),
        'norm_final': gain((D_MODEL,)),
        'ev_w_in': nrm((N_EVEN, D_MODEL, EVEN_IN), D_MODEL ** -0.5),
        'ev_w_out': nrm((N_EVEN, MIX_WIDTH, D_MODEL), MIX_WIDTH ** -0.5),
        'gdn_conv_w': nrm((N_EVEN, CONV_WIDTH, GDN_CONV_CH), CONV_WIDTH ** -0.5),
        'gdn_a_log': jnp.log(unif((N_EVEN, GDN_HEADS), 1.0, 16.0)),
        'gdn_dt_bias': dt + jnp.log(-jnp.expm1(-dt)),
        'gdn_norm': gain((N_EVEN, GDN_HEAD_DIM)),
        'rwkv_mu': unif((N_EVEN, RWKV_IN), 0.0, 1.0),
        'rwkv_w0': unif((N_EVEN, RWKV_WIDTH), -6.0, -1.0),
        'rwkv_w2': nrm((N_EVEN, DECAY_LORA, RWKV_WIDTH), 0.1 * DECAY_LORA ** -0.5),
        'rwkv_a0': nrm((N_EVEN, RWKV_WIDTH), 0.1),
        'rwkv_a2': nrm((N_EVEN, ICLR_LORA, RWKV_WIDTH), 0.5 * ICLR_LORA ** -0.5),
        'rwkv_g2': nrm((N_EVEN, GATE_LORA, RWKV_WIDTH), GATE_LORA ** -0.5),
        'rwkv_k_k': 0.85 + nrm((N_EVEN, RWKV_WIDTH), 0.02),
        'rwkv_k_a': 1.0 + nrm((N_EVEN, RWKV_WIDTH), 0.02),
        'rwkv_r_k': nrm((N_EVEN, RWKV_WIDTH), 0.1),
        'rwkv_gn_g': gain((N_EVEN, RWKV_WIDTH)),
        'rwkv_gn_b': nrm((N_EVEN, RWKV_WIDTH), 0.01),
        'sb_w_in': nrm((N_ODD, D_MODEL, 3 * MIX_WIDTH), D_MODEL ** -0.5),
        'sb_w_out': nrm((N_ODD, MIX_WIDTH, D_MODEL), MIX_WIDTH ** -0.5),
        'sb_bias': SB_BIAS_INIT + nrm((N_ODD, SB_HEADS), 0.5),
        'mem_w_q': nrm((DEPTH, D_MODEL, D_MODEL), D_MODEL ** -0.5),
        'mem_w_k': nrm((DEPTH, D_MODEL, D_MODEL), D_MODEL ** -0.5),
        'mem_w_v': nrm((DEPTH, D_MODEL, D_MODEL), D_MODEL ** -0.5),
        'mem_w_o': nrm((DEPTH, D_MODEL, D_MODEL), D_MODEL ** -0.5),
        'ffn_w_in': nrm((DEPTH, D_MODEL, 2 * D_FF), D_MODEL ** -0.5),
        'ffn_w_out': nrm((DEPTH, D_FF, D_MODEL), D_FF ** -0.5),
    }


def reference(x_prompt, x_sample, mem_prompt, state_gdn, state_gdn_conv, state_rwkv, state_rwkv_shift,
              cache_sb_k, cache_sb_v, cache_mem_k, cache_mem_v, page_table,
              norm_mix, norm_mem, norm_memtok, norm_ffn, norm_final, ev_w_in, ev_w_out,
              gdn_conv_w, gdn_a_log, gdn_dt_bias, gdn_norm, rwkv_mu, rwkv_w0, rwkv_w2, rwkv_a0, rwkv_a2, rwkv_g2,
              rwkv_k_k, rwkv_k_a, rwkv_r_k, rwkv_gn_g, rwkv_gn_b, sb_w_in, sb_w_out, sb_bias,
              mem_w_q, mem_w_k, mem_w_v, mem_w_o, ffn_w_in, ffn_w_out):
    weights = (norm_mix, norm_mem, norm_ffn, norm_final, ev_w_in, ev_w_out, gdn_conv_w, gdn_a_log, gdn_dt_bias,
               gdn_norm, rwkv_mu, rwkv_w0, rwkv_w2, rwkv_a0, rwkv_a2, rwkv_g2, rwkv_k_k, rwkv_k_a, rwkv_r_k,
               rwkv_gn_g, rwkv_gn_b, sb_w_in, sb_w_out, sb_bias, mem_w_q, mem_w_o, ffn_w_in, ffn_w_out)
    f32 = jnp.float32
    bp = x_prompt.shape[0]
    mk_list, mv_list = [], []
    for layer in range(DEPTH):
        mk, mv = mem_kv(mem_prompt, norm_memtok[layer], mem_w_k[layer], mem_w_v[layer])
        mk_list.append(mk)
        mv_list.append(mv)
    mem_k_p = jnp.stack(mk_list)
    mem_v_p = jnp.stack(mv_list)
    empty = jnp.zeros((bp, 0, SB_HEADS, SB_HEAD_DIM), x_prompt.dtype)
    y_p, conv_p, gdn_p, rwkv_p, shift_p, sbk_p, sbv_p = run_trunk(
        x_prompt, mem_k_p, mem_v_p,
        jnp.zeros((N_EVEN, bp, CONV_WIDTH - 1, GDN_CONV_CH), x_prompt.dtype),
        jnp.zeros((N_EVEN, bp, GDN_HEADS, GDN_HEAD_DIM, GDN_HEAD_DIM), f32),
        jnp.zeros((N_EVEN, bp, RWKV_HEADS, RWKV_HEAD_DIM, RWKV_HEAD_DIM), f32),
        jnp.zeros((N_EVEN, bp, RWKV_IN), x_prompt.dtype),
        lambda i: (empty, empty), weights)
    y_s, conv_s, gdn_s, rwkv_s, shift_s, sbk_s, sbv_s = run_trunk(
        x_sample, cache_mem_k, cache_mem_v, state_gdn_conv, state_gdn, state_rwkv, state_rwkv_shift,
        lambda i: (gather_pages(cache_sb_k[i], page_table), gather_pages(cache_sb_v[i], page_table)), weights)
    return (y_p, y_s, gdn_p, gdn_s, conv_p, conv_s, rwkv_p, rwkv_s, shift_p, shift_s,
            sbk_p, sbk_s, sbv_p, sbv_s, mem_k_p, mem_v_p)
```

```python
import functools
import math

import jax
import jax.numpy as jnp
from jax import lax
from jax.experimental import pallas as pl
from jax.experimental.pallas import tpu as pltpu

F32 = jnp.float32
BF16 = jnp.bfloat16

EPS = 1e-6
RWKV_GN_EPS = 64e-5
CONV_WIDTH = 4
GDN_HEADS = 4
GDN_HEAD_DIM = 128
GDN_WIDTH = GDN_HEADS * GDN_HEAD_DIM
GDN_CONV_CH = 3 * GDN_WIDTH
RWKV_HEAD_DIM = 64
RWKV_WIDTH = 512
RWKV_PAIRS = RWKV_WIDTH // 128
RWKV_IN = 1792
SB_HEADS = 16
SB_HEAD_DIM = 64
MEM_HEADS = 4
PAGE_SIZE = 128
LANES = 128
SUBLANES = 8
V7X_VMEM_LIMIT_BYTES = 56 * 1024 * 1024
PAGES_PER_STEP = 4


def _params(*sem):
    return pltpu.CompilerParams(dimension_semantics=sem, vmem_limit_bytes=V7X_VMEM_LIMIT_BYTES)


def _mm(a, b):
    return jnp.dot(a.astype(BF16), b.astype(BF16), preferred_element_type=F32)


def _mm_nt(a, b):
    return lax.dot_general(a.astype(BF16), b.astype(BF16), (((1,), (1,)), ((), ())),
                           preferred_element_type=F32)


def _mm_tn(a, b):
    return lax.dot_general(a.astype(BF16), b.astype(BF16), (((0,), (0,)), ((), ())),
                           preferred_element_type=F32)


def _hi_lo(x):
    hi = x.astype(BF16)
    lo = (x - hi.astype(F32)).astype(BF16)
    return hi, lo


def _mm3(a, b):
    ah, al = _hi_lo(a)
    bh, bl = _hi_lo(b)
    d = functools.partial(jnp.dot, preferred_element_type=F32)
    return d(ah, bh) + d(ah, bl) + d(al, bh)


def _mm_exact_lhs(a_bf, x):
    xh, xl = _hi_lo(x)
    return (jnp.dot(a_bf, xh, preferred_element_type=F32)
            + jnp.dot(a_bf, xl, preferred_element_type=F32))


def _mm_exact_rhs(x, b_bf):
    xh, xl = _hi_lo(x)
    return (jnp.dot(xh, b_bf, preferred_element_type=F32)
            + jnp.dot(xl, b_bf, preferred_element_type=F32))


def _rms(x, g):
    return x * lax.rsqrt(jnp.mean(x * x, axis=-1, keepdims=True) + EPS) * g


def _softplus(x):
    return jnp.maximum(x, 0.0) + jnp.log1p(jnp.exp(-jnp.abs(x)))


def _iota2(shape, dim):
    return lax.broadcasted_iota(jnp.int32, shape, dim)


def _row_tile(m, want):
    t = min(m, want)
    while m % t:
        t //= 2
    return t


def _levels(c):
    return max(1, int(math.ceil(math.log2(c))))


def _norm_proj_kernel(x_ref, g_ref, *refs, n_out):
    xn = _rms(x_ref[...], g_ref[...]).astype(BF16)
    for w_ref, o_ref in zip(refs[:n_out], refs[n_out:]):
        o_ref[...] = jnp.dot(xn, w_ref[...], preferred_element_type=F32)


def _norm_proj(x2d, g, ws, tm=512):
    m, d = x2d.shape
    tm = _row_tile(m, tm)
    n_out = len(ws)
    in_specs = [pl.BlockSpec((tm, d), lambda i: (i, 0)), pl.BlockSpec((1, d), lambda i: (0, 0))]
    in_specs += [pl.BlockSpec(w.shape, lambda i: (0, 0)) for w in ws]
    out_specs = [pl.BlockSpec((tm, w.shape[1]), lambda i: (i, 0)) for w in ws]
    out_shape = [jax.ShapeDtypeStruct((m, w.shape[1]), F32) for w in ws]
    return pl.pallas_call(
        functools.partial(_norm_proj_kernel, n_out=n_out),
        grid=(m // tm,), in_specs=in_specs, out_specs=out_specs, out_shape=out_shape,
        compiler_params=_params("parallel"), name="norm_proj",
    )(x2d, g.reshape(1, d), *ws)


def _proj_res_kernel(res_ref, *refs, n_in):
    acc = res_ref[...]
    for a_ref, w_ref in zip(refs[:n_in], refs[n_in:2 * n_in]):
        acc = acc + jnp.dot(a_ref[...].astype(BF16), w_ref[...], preferred_element_type=F32)
    refs[2 * n_in][...] = acc


def _proj_residual(res2d, acts, ws, tm=512):
    m, d = res2d.shape
    tm = _row_tile(m, tm)
    n_in = len(acts)
    in_specs = [pl.BlockSpec((tm, d), lambda i: (i, 0))]
    in_specs += [pl.BlockSpec((tm, a.shape[1]), lambda i: (i, 0)) for a in acts]
    in_specs += [pl.BlockSpec(w.shape, lambda i: (0, 0)) for w in ws]
    return pl.pallas_call(
        functools.partial(_proj_res_kernel, n_in=n_in),
        grid=(m // tm,), in_specs=in_specs,
        out_specs=pl.BlockSpec((tm, d), lambda i: (i, 0)),
        out_shape=jax.ShapeDtypeStruct((m, d), F32),
        compiler_params=_params("parallel"), name="proj_residual",
    )(res2d, *acts, *ws)


def _ffn_kernel(x_ref, g_ref, wg_ref, wu_ref, wo_ref, gf_ref, o_ref, xn_ref, acc_ref, *, final_norm):
    j = pl.program_id(1)

    @pl.when(j == 0)
    def _():
        xn_ref[...] = _rms(x_ref[...], g_ref[...]).astype(BF16)
        acc_ref[...] = jnp.zeros_like(acc_ref)

    xn = xn_ref[...]
    gt = jnp.dot(xn, wg_ref[...], preferred_element_type=F32)
    ut = jnp.dot(xn, wu_ref[...], preferred_element_type=F32)
    act = (gt * jax.nn.sigmoid(gt) * ut).astype(BF16)
    acc_ref[...] += jnp.dot(act, wo_ref[...], preferred_element_type=F32)

    @pl.when(j == pl.num_programs(1) - 1)
    def _():
        y = x_ref[...] + acc_ref[...]
        if final_norm:
            y = _rms(y, gf_ref[...])
        o_ref[...] = y


def _ffn(x2d, g, w_in, w_out, g_final, final_norm, tm=512, n_chunks=2):
    m, d = x2d.shape
    dff = w_out.shape[0]
    tf = dff // n_chunks
    tm = _row_tile(m, tm)
    return pl.pallas_call(
        functools.partial(_ffn_kernel, final_norm=final_norm),
        grid=(m // tm, n_chunks),
        in_specs=[
            pl.BlockSpec((tm, d), lambda i, j: (i, 0)),
            pl.BlockSpec((1, d), lambda i, j: (0, 0)),
            pl.BlockSpec((d, tf), lambda i, j: (0, j)),
            pl.BlockSpec((d, tf), lambda i, j: (0, n_chunks + j)),
            pl.BlockSpec((tf, d), lambda i, j: (j, 0)),
            pl.BlockSpec((1, d), lambda i, j: (0, 0)),
        ],
        out_specs=pl.BlockSpec((tm, d), lambda i, j: (i, 0)),
        out_shape=jax.ShapeDtypeStruct((m, d), F32),
        scratch_shapes=[pltpu.VMEM((tm, d), BF16), pltpu.VMEM((tm, d), F32)],
        compiler_params=_params("parallel", "arbitrary"), name="ffn",
    )(x2d, g.reshape(1, d), w_in, w_in, w_out, g_final.reshape(1, d))


def _mem_attn_kernel(x_ref, g_ref, wq_ref, wo_ref, mk_ref, mv_ref, o_ref, *, scale, heads):
    x = x_ref[0]
    xn = _rms(x, g_ref[...]).astype(BF16)
    q = (jnp.dot(xn, wq_ref[...], preferred_element_type=F32) * scale).astype(BF16)
    mk = mk_ref[0].astype(BF16)
    mv = mv_ref[0].astype(BF16)
    hd = x.shape[-1] // heads
    outs = []
    for h in range(heads):
        sl = slice(h * hd, (h + 1) * hd)
        s = _mm_nt(q[:, sl], mk[:, sl])
        e = jnp.exp(s - jnp.max(s, axis=-1, keepdims=True))
        outs.append(_mm(e, mv[:, sl]) / jnp.sum(e, axis=-1, keepdims=True))
    o = jnp.concatenate(outs, axis=-1).astype(BF16)
    o_ref[0] = x + jnp.dot(o, wo_ref[...], preferred_element_type=F32)


def _mem_attn(x, g, wq, wo, mk, mv, mem_base, tm=512):
    b, t, d = x.shape
    tm = _row_tile(t, tm)
    n_mem = mk.shape[1]
    return pl.pallas_call(
        functools.partial(_mem_attn_kernel, scale=(d // MEM_HEADS) ** -0.5, heads=MEM_HEADS),
        grid=(b, t // tm),
        in_specs=[
            pl.BlockSpec((1, tm, d), lambda i, j: (i, j, 0)),
            pl.BlockSpec((1, d), lambda i, j: (0, 0)),
            pl.BlockSpec((d, d), lambda i, j: (0, 0)),
            pl.BlockSpec((d, d), lambda i, j: (0, 0)),
            pl.BlockSpec((1, n_mem, d), lambda i, j: (mem_base + i, 0, 0)),
            pl.BlockSpec((1, n_mem, d), lambda i, j: (mem_base + i, 0, 0)),
        ],
        out_specs=pl.BlockSpec((1, tm, d), lambda i, j: (i, j, 0)),
        out_shape=jax.ShapeDtypeStruct((b, t, d), F32),
        compiler_params=_params("parallel", "parallel"), name="mem_attn",
    )(x, g.reshape(1, d), wq, wo, mk, mv)


def _gdn_kernel(gp_ref, bg_ref, cb_ref, s0_ref, cw_ref, hp_ref, gn_ref, o_ref, so_ref, cv_ref,
                ext_ref, s_ref, *, chunk, levels):
    c = pl.program_id(1)
    C = chunk
    dk = GDN_HEAD_DIM
    tail = CONV_WIDTH - 1

    @pl.when(c == 0)
    def _():
        ext_ref[SUBLANES - tail:SUBLANES, :] = cb_ref[0]
        s_ref[...] = s0_ref[0]

    gp = gp_ref[0]
    u_in = gp[:, :GDN_CONV_CH]
    ext_ref[SUBLANES:SUBLANES + C, :] = u_in
    cw = cw_ref[...]
    conv = u_in * cw[tail:tail + 1]
    for j in range(tail):
        lo = SUBLANES - tail + j
        conv = conv + ext_ref[lo:lo + C, :] * cw[j:j + 1]
    new_tail = ext_ref[SUBLANES + C - tail:SUBLANES + C, :]
    ext_ref[SUBLANES - tail:SUBLANES, :] = new_tail
    cv_ref[0] = new_tail
    qkv = conv * jax.nn.sigmoid(conv)

    bg = bg_ref[0]
    lane = _iota2(bg.shape, 1)
    beta_all = jax.nn.sigmoid(bg)
    g_all = -jnp.exp(hp_ref[0:1, :]) * _softplus(bg + hp_ref[1:2, :])

    row = _iota2((C, C), 0)
    col = _iota2((C, C), 1)
    tri = (row >= col).astype(BF16)
    ones_cc = jnp.ones((C, C), BF16)
    eye = (row == col).astype(F32)

    for h in range(GDN_HEADS):
        sl = slice(h * dk, (h + 1) * dk)
        qh = qkv[:, sl]
        kh = qkv[:, GDN_WIDTH + h * dk:GDN_WIDTH + (h + 1) * dk]
        vh = qkv[:, 2 * GDN_WIDTH + h * dk:2 * GDN_WIDTH + (h + 1) * dk]
        zh = gp[:, GDN_CONV_CH + h * dk:GDN_CONV_CH + (h + 1) * dk]
        qn = qh * lax.rsqrt(jnp.sum(qh * qh, axis=-1, keepdims=True) + EPS) * (dk ** -0.5)
        kn = kh * lax.rsqrt(jnp.sum(kh * kh, axis=-1, keepdims=True) + EPS)
        beta = jnp.sum(jnp.where(lane == h, beta_all, 0.0), axis=-1, keepdims=True)
        g = jnp.sum(jnp.where(lane == GDN_HEADS + h, g_all, 0.0), axis=-1, keepdims=True)

        gcum = _mm_exact_lhs(tri, jnp.broadcast_to(g, (C, dk)))
        g_cc = jnp.broadcast_to(g, (C, C))
        gcum_i = _mm_exact_lhs(tri, g_cc)
        gcum_j = _mm_exact_lhs(ones_cc, jnp.where(row <= col, g_cc, 0.0))
        decay = jnp.where(row >= col, jnp.exp(jnp.minimum(gcum_i - gcum_j, 0.0)), 0.0)
        gam = jnp.exp(gcum)
        glast = gcum[C - 1:C, :]
        kdec = kn * jnp.exp(glast - gcum)

        kkt = _mm_nt(kn, kn)
        qkt = _mm_nt(qn, kn)
        x = jnp.where(row > col, -(beta * kkt * decay), 0.0)
        inv = eye + x
        pw = x
        for _ in range(levels - 1):
            pw = _mm3(pw, pw)
            inv = inv + _mm3(inv, pw)
        w_mat = _mm(inv, beta * gam * kn)
        u0 = _mm(inv, beta * vh)

        s = s_ref[h]
        u = u0 - _mm(w_mat, s)
        o = gam * _mm(qn, s) + _mm(qkt * decay, u)
        s_ref[h] = jnp.exp(glast) * s + _mm_tn(kdec, u)

        o_ref[0, :, sl] = _rms(o, gn_ref[...]) * (zh * jax.nn.sigmoid(zh))

    @pl.when(c == pl.num_programs(1) - 1)
    def _():
        so_ref[0] = s_ref[...]


def _gdn(gp, bg, conv_buf, s0, conv_w, head_params, gnorm):
    b, t, _ = gp.shape
    chunk = min(64, t)
    assert t % chunk == 0 and chunk % SUBLANES == 0 and chunk >= CONV_WIDTH - 1
    return pl.pallas_call(
        functools.partial(_gdn_kernel, chunk=chunk, levels=_levels(chunk)),
        grid=(b, t // chunk),
        in_specs=[
            pl.BlockSpec((1, chunk, gp.shape[2]), lambda i, j: (i, j, 0)),
            pl.BlockSpec((1, chunk, LANES), lambda i, j: (i, j, 0)),
            pl.BlockSpec((1, CONV_WIDTH - 1, GDN_CONV_CH), lambda i, j: (i, 0, 0)),
            pl.BlockSpec((1, GDN_HEADS, GDN_HEAD_DIM, GDN_HEAD_DIM), lambda i, j: (i, 0, 0, 0)),
            pl.BlockSpec((CONV_WIDTH, GDN_CONV_CH), lambda i, j: (0, 0)),
            pl.BlockSpec((SUBLANES, LANES), lambda i, j: (0, 0)),
            pl.BlockSpec((1, GDN_HEAD_DIM), lambda i, j: (0, 0)),
        ],
        out_specs=[
            pl.BlockSpec((1, chunk, GDN_WIDTH), lambda i, j: (i, j, 0)),
            pl.BlockSpec((1, GDN_HEADS, GDN_HEAD_DIM, GDN_HEAD_DIM), lambda i, j: (i, 0, 0, 0)),
            pl.BlockSpec((1, CONV_WIDTH - 1, GDN_CONV_CH), lambda i, j: (i, 0, 0)),
        ],
        out_shape=[
            jax.ShapeDtypeStruct((b, t, GDN_WIDTH), F32),
            jax.ShapeDtypeStruct((b, GDN_HEADS, GDN_HEAD_DIM, GDN_HEAD_DIM), F32),
            jax.ShapeDtypeStruct((b, CONV_WIDTH - 1, GDN_CONV_CH), F32),
        ],
        scratch_shapes=[
            pltpu.VMEM((chunk + SUBLANES, GDN_CONV_CH), F32),
            pltpu.VMEM((GDN_HEADS, GDN_HEAD_DIM, GDN_HEAD_DIM), F32),
        ],
        compiler_params=_params("parallel", "arbitrary"), name="gdn",
    )(gp, bg, conv_buf, s0, conv_w, head_params, gnorm.reshape(1, GDN_HEAD_DIM))


def _rwkv_kernel(rp_ref, sh_ref, s0_ref, mu_ref, vec_ref, w2a_ref, g2_ref, o_ref, so_ref, sho_ref,
                 last_ref, s_ref, *, chunk, levels):
    c = pl.program_id(1)
    C = chunk
    W = RWKV_WIDTH
    hd = RWKV_HEAD_DIM

    @pl.when(c == 0)
    def _():
        last_ref[...] = sh_ref[0]
        s_ref[...] = s0_ref[0]

    rp = rp_ref[0]
    rowi = _iota2(rp.shape, 0)
    prev = jnp.where(rowi == 0, last_ref[...], pltpu.roll(rp, 1, 0))
    xr = rp + (prev - rp) * mu_ref[...]
    last_row = rp[C - 1:C, :]
    last_ref[...] = last_row
    sho_ref[0] = last_row

    vec = vec_ref[...]
    r_all = xr[:, :W]
    kr = xr[:, W:2 * W]
    v_all = xr[:, 2 * W:3 * W]
    pwa = xr[:, 3 * W:3 * W + LANES]
    pg = xr[:, 3 * W + LANES:]
    lane = _iota2((C, LANES), 1)
    m0 = lane < hd
    wa = _mm(jnp.where(m0, jnp.tanh(pwa), pwa), w2a_ref[...])
    w_raw = vec[0:1] + wa[:, :W]
    a_all = jax.nn.sigmoid(vec[1:2] + wa[:, W:])
    lw_all = -jnp.exp(-_softplus(-w_raw) - 0.5)
    gate = _mm(jax.nn.sigmoid(pg), g2_ref[...])
    kkp = kr * vec[2:3]
    k2_all = kr * (1.0 + (a_all - 1.0) * vec[3:4])

    r128 = _iota2((LANES, LANES), 0)
    c128 = _iota2((LANES, LANES), 1)
    bd_mask = (r128 >= hd) == (c128 >= hd)
    bd = bd_mask.astype(BF16)
    rc = _iota2((C, C), 0)
    cc = _iota2((C, C), 1)
    tri = (rc >= cc).astype(BF16)
    r2 = _iota2((2 * C, 2 * C), 0)
    c2 = _iota2((2 * C, 2 * C), 1)
    same = (r2 >= C) == (c2 >= C)
    strict = same & (r2 > c2)
    incl = same & (r2 >= c2)
    eye2 = (r2 == c2).astype(F32)

    def stack(x):
        return jnp.concatenate([x, x], axis=0)

    def stack_masked(x):
        return jnp.concatenate([jnp.where(m0, x, 0.0), jnp.where(m0, 0.0, x)], axis=0)

    def sel(z):
        return jnp.where(m0, z[:C], z[C:])

    for p in range(RWKV_PAIRS):
        sl = slice(p * LANES, (p + 1) * LANES)
        r = r_all[:, sl]
        k2 = k2_all[:, sl]
        v = v_all[:, sl]
        lw = lw_all[:, sl]
        a = a_all[:, sl]
        kk0 = kkp[:, sl]
        kk = kk0 * lax.rsqrt(_mm_exact_rhs(kk0 * kk0, bd) + EPS)

        gcum = _mm_exact_lhs(tri, lw)
        e_in = jnp.exp(gcum)
        e_ex = jnp.exp(gcum - lw)
        e_inv = jnp.exp(-gcum)
        abar = -kk * e_ex
        bbar = kk * a * e_inv
        kbar = k2 * e_inv
        rbar = r * e_in

        a_s = stack_masked(abar)
        r_s = stack_masked(rbar)
        b_c = stack(bbar)
        k_c = stack(kbar)
        v_c = stack(v)
        l_ab = jnp.where(strict, _mm_nt(a_s, b_c), 0.0)
        l_ak = jnp.where(strict, _mm_nt(a_s, k_c), 0.0)
        m_rb = jnp.where(incl, _mm_nt(r_s, b_c), 0.0)
        m_rk = jnp.where(incl, _mm_nt(r_s, k_c), 0.0)
        inv = eye2 + l_ab
        pw = l_ab
        for _ in range(levels - 1):
            pw = _mm3(pw, pw)
            inv = inv + _mm3(inv, pw)

        s = s_ref[p]
        ar = _mm_nt(jnp.concatenate([abar, rbar], axis=0), s)
        rhs = ar[:C] + sel(_mm(l_ak, v_c))
        u = sel(_mm(inv, stack(rhs)))
        y = ar[C:] + sel(_mm(m_rb, stack(u)) + _mm(m_rk, v_c))
        upd = _mm_tn(jnp.concatenate([u, v], axis=0), jnp.concatenate([bbar, kbar], axis=0))
        s_ref[p] = (s + jnp.where(bd_mask, upd, 0.0)) * e_in[C - 1:C, :]

        inv_n = 1.0 / hd
        mean = _mm_exact_rhs(y, bd) * inv_n
        yc = y - mean
        var = _mm_exact_rhs(yc * yc, bd) * inv_n
        yn = yc * lax.rsqrt(var + RWKV_GN_EPS) * vec[5:6, sl] + vec[6:7, sl]
        bonus = _mm_exact_rhs(r * k2 * vec[4:5, sl], bd) * v
        o_ref[0, :, sl] = (yn + bonus) * gate[:, sl]

    @pl.when(c == pl.num_programs(1) - 1)
    def _():
        so_ref[0] = s_ref[...]


def _rwkv(rp, shift, s0_pairs, mu, vecs, w2a, g2):
    b, t, _ = rp.shape
    chunk = min(64, t)
    assert t % chunk == 0 and chunk % SUBLANES == 0
    return pl.pallas_call(
        functools.partial(_rwkv_kernel, chunk=chunk, levels=_levels(chunk)),
        grid=(b, t // chunk),
        in_specs=[
            pl.BlockSpec((1, chunk, RWKV_IN), lambda i, j: (i, j, 0)),
            pl.BlockSpec((1, 1, RWKV_IN), lambda i, j: (i, 0, 0)),
            pl.BlockSpec((1, RWKV_PAIRS, LANES, LANES), lambda i, j: (i, 0, 0, 0)),
            pl.BlockSpec((1, RWKV_IN), lambda i, j: (0, 0)),
            pl.BlockSpec((SUBLANES, RWKV_WIDTH), lambda i, j: (0, 0)),
            pl.BlockSpec(w2a.shape, lambda i, j: (0, 0)),
            pl.BlockSpec(g2.shape, lambda i, j: (0, 0)),
        ],
        out_specs=[
            pl.BlockSpec((1, chunk, RWKV_WIDTH), lambda i, j: (i, j, 0)),
            pl.BlockSpec((1, RWKV_PAIRS, LANES, LANES), lambda i, j: (i, 0, 0, 0)),
            pl.BlockSpec((1, 1, RWKV_IN), lambda i, j: (i, 0, 0)),
        ],
        out_shape=[
            jax.ShapeDtypeStruct((b, t, RWKV_WIDTH), F32),
            jax.ShapeDtypeStruct((b, RWKV_PAIRS, LANES, LANES), F32),
            jax.ShapeDtypeStruct((b, 1, RWKV_IN), F32),
        ],
        scratch_shapes=[
            pltpu.VMEM((1, RWKV_IN), F32),
            pltpu.VMEM((RWKV_PAIRS, LANES, LANES), F32),
        ],
        compiler_params=_params("parallel", "arbitrary"), name="rwkv7",
    )(rp, shift.reshape(b, 1, RWKV_IN), s0_pairs, mu.reshape(1, RWKV_IN), vecs, w2a, g2)


def _sb_tile(z, r_later, u_bf, mask):
    t = jnp.log1p(jnp.exp(-jnp.abs(z)))
    lneg = jnp.minimum(-z, 0.0) - t
    lpos = jnp.minimum(z, 0.0) - t
    if mask is not None:
        lneg = jnp.where(mask, lneg, 0.0)
    cum = _mm_exact_rhs(lneg, u_bf)
    att = jnp.exp(lpos + (cum - lneg) + r_later)
    if mask is not None:
        att = jnp.where(mask, att, 0.0)
    return att, r_later + jnp.sum(lneg, axis=-1, keepdims=True)


def _sb_prompt_kernel(bias_ref, q_ref, k_ref, v_ref, o_ref, *, scale, blk):
    p = pl.program_id(1)
    qi = pl.program_id(2)
    hd = SB_HEAD_DIM
    q = q_ref[0] * scale
    lane = _iota2((blk, LANES), 1)
    m0 = lane < hd
    qm = (jnp.where(m0, q, 0.0).astype(BF16), jnp.where(m0, 0.0, q).astype(BF16))
    bias = (bias_ref[2 * p], bias_ref[2 * p + 1])
    rr = _iota2((blk, blk), 0)
    cc = _iota2((blk, blk), 1)
    u_bf = (rr >= cc).astype(BF16)
    causal = cc < rr

    def tile(kj, carry, mask):
        start = pl.multiple_of(kj * blk, blk)
        kb = k_ref[0, pl.ds(start, blk), :].astype(BF16)
        vb = v_ref[0, pl.ds(start, blk), :].astype(BF16)
        out = []
        for h in range(2):
            r_later, acc = carry[h]
            att, r_later = _sb_tile(_mm_nt(qm[h], kb) + bias[h], r_later, u_bf, mask)
            out.append((r_later, acc + jnp.dot(att.astype(BF16), vb, preferred_element_type=F32)))
        return tuple(out)

    init = tuple((jnp.zeros((blk, 1), F32), jnp.zeros((blk, LANES), F32)) for _ in range(2))
    carry = tile(qi, init, causal)
    carry = lax.fori_loop(0, qi, lambda i, cr: tile(qi - 1 - i, cr, None), carry)
    o_ref[0] = jnp.where(m0, carry[0][1], carry[1][1])


def _sb_prompt(q, k, v, bias):
    b, t, d = q.shape
    blk = 128 if t % 128 == 0 else t
    assert blk % SUBLANES == 0
    return pl.pallas_call(
        functools.partial(_sb_prompt_kernel, scale=SB_HEAD_DIM ** -0.5, blk=blk),
        grid=(b, d // LANES, t // blk),
        in_specs=[
            pl.BlockSpec(memory_space=pltpu.SMEM),
            pl.BlockSpec((1, blk, LANES), lambda i, p, j: (i, j, p)),
            pl.BlockSpec((1, t, LANES), lambda i, p, j: (i, 0, p)),
            pl.BlockSpec((1, t, LANES), lambda i, p, j: (i, 0, p)),
        ],
        out_specs=pl.BlockSpec((1, blk, LANES), lambda i, p, j: (i, j, p)),
        out_shape=jax.ShapeDtypeStruct((b, t, d), F32),
        compiler_params=_params("parallel", "parallel", "arbitrary"), name="sb_prompt",
    )(bias, q, k, v)


def _sb_paged_kernel(pt_ref, bias_ref, q_ref, kn_ref, vn_ref, *refs, scale, pps, tq):
    k_refs = refs[:pps]
    v_refs = refs[pps:2 * pps]
    o_ref, qx_ref, acc_ref, r_ref = refs[2 * pps:]
    s = pl.program_id(1)
    hd = SB_HEAD_DIM
    rows = SB_HEADS * tq
    d = SB_HEADS * hd

    @pl.when(s == 0)
    def _():
        q = q_ref[0] * scale
        qt = jnp.concatenate([q] * SB_HEADS, axis=0)
        rh = _iota2((rows, d), 0) // tq
        lh = _iota2((rows, d), 1) // hd
        qx_ref[...] = jnp.where(rh == lh, qt, 0.0).astype(BF16)
        z = _mm_nt(qx_ref[...], kn_ref[0]) + bias_ref[:, :tq]
        rr = _iota2((rows, tq), 0)
        cc = _iota2((rows, tq), 1)
        mask = cc < (rr & (tq - 1))
        ur = _iota2((tq, tq), 0)
        uc = _iota2((tq, tq), 1)
        att, r_later = _sb_tile(z, jnp.zeros((rows, 1), F32), (ur >= uc).astype(BF16), mask)
        acc_ref[...] = _mm(att, vn_ref[0])
        r_ref[...] = jnp.broadcast_to(r_later, r_ref.shape)

    ur = _iota2((PAGE_SIZE, PAGE_SIZE), 0)
    uc = _iota2((PAGE_SIZE, PAGE_SIZE), 1)
    u_bf = (ur >= uc).astype(BF16)
    qx = qx_ref[...]
    r_later = r_ref[...]
    acc = acc_ref[...]
    for j in range(pps):
        z = _mm_nt(qx, k_refs[j][0]) + bias_ref[...]
        att, r_later = _sb_tile(z, r_later, u_bf, None)
        acc = acc + _mm(att, v_refs[j][0])
    acc_ref[...] = acc
    r_ref[...] = r_later

    @pl.when(s == pl.num_programs(1) - 1)
    def _():
        lh = _iota2((tq, d), 1) // hd
        out = jnp.zeros((tq, d), F32)
        for h in range(SB_HEADS):
            out = out + jnp.where(lh == h, acc[h * tq:(h + 1) * tq, :], 0.0)
        o_ref[0] = out


def _sb_paged(q, k_new, v_new, pool_k, pool_v, page_table, pool_base, bias):
    b, tq, d = q.shape
    n_pages = page_table.shape[1]
    pps = PAGES_PER_STEP
    while n_pages % pps:
        pps //= 2
    assert tq & (tq - 1) == 0 and (SB_HEADS * tq) % SUBLANES == 0
    rows = SB_HEADS * tq
    bias_rows = jnp.broadcast_to(jnp.repeat(bias, tq)[:, None], (rows, PAGE_SIZE))

    def page_map(j):
        return lambda i, s, pt: (pool_base + pt[i, n_pages - 1 - (s * pps + j)], 0, 0)

    page_specs = [pl.BlockSpec((1, PAGE_SIZE, d), page_map(j)) for j in range(pps)]
    row_spec = pl.BlockSpec((1, tq, d), lambda i, s, pt: (i, 0, 0))
    grid_spec = pltpu.PrefetchScalarGridSpec(
        num_scalar_prefetch=1,
        grid=(b, n_pages // pps),
        in_specs=[pl.BlockSpec((rows, PAGE_SIZE), lambda i, s, pt: (0, 0)), row_spec, row_spec, row_spec]
        + page_specs + page_specs,
        out_specs=row_spec,
        scratch_shapes=[pltpu.VMEM((rows, d), BF16), pltpu.VMEM((rows, d), F32),
                        pltpu.VMEM((rows, PAGE_SIZE), F32)],
    )
    return pl.pallas_call(
        functools.partial(_sb_paged_kernel, scale=SB_HEAD_DIM ** -0.5, pps=pps, tq=tq),
        grid_spec=grid_spec,
        out_shape=jax.ShapeDtypeStruct((b, tq, d), F32),
        compiler_params=_params("parallel", "arbitrary"), name="sb_paged",
    )(page_table, bias_rows, q, k_new, v_new, *([pool_k] * pps), *([pool_v] * pps))


def _rwkv_state_to_pairs(s):
    b = s.shape[0]
    s = s.reshape(b, RWKV_PAIRS, 2, RWKV_HEAD_DIM, RWKV_HEAD_DIM)
    z = jnp.zeros_like(s[:, :, 0])
    top = jnp.concatenate([s[:, :, 0], z], axis=-1)
    bot = jnp.concatenate([z, s[:, :, 1]], axis=-1)
    return jnp.concatenate([top, bot], axis=-2)


def _rwkv_state_from_pairs(sp):
    b = sp.shape[0]
    hd = RWKV_HEAD_DIM
    return jnp.stack([sp[:, :, :hd, :hd], sp[:, :, hd:, hd:]], axis=2).reshape(b, 2 * RWKV_PAIRS, hd, hd)


def _prep_weights(p):
    w = {}
    ev = p["ev_w_in"][0]
    gdn_in = 4 * GDN_WIDTH
    w["ev_g"] = ev[:, :gdn_in].astype(BF16)
    w["ev_r"] = ev[:, gdn_in + 2 * GDN_HEADS:].astype(BF16)
    w["ev_bg"] = jnp.pad(ev[:, gdn_in:gdn_in + 2 * GDN_HEADS],
                         ((0, 0), (0, LANES - 2 * GDN_HEADS))).astype(BF16)
    w_out = p["ev_w_out"][0].astype(BF16)
    w["ev_out_a"], w["ev_out_b"] = w_out[:GDN_WIDTH], w_out[GDN_WIDTH:]
    hp = jnp.zeros((SUBLANES, LANES), F32)
    hp = hp.at[0, GDN_HEADS:2 * GDN_HEADS].set(p["gdn_a_log"][0])
    hp = hp.at[1, GDN_HEADS:2 * GDN_HEADS].set(p["gdn_dt_bias"][0])
    w["gdn_hp"] = hp
    names = ("rwkv_w0", "rwkv_a0", "rwkv_k_k", "rwkv_k_a", "rwkv_r_k", "rwkv_gn_g", "rwkv_gn_b")
    rows = [p[n][0] for n in names] + [jnp.zeros((RWKV_WIDTH,), F32)]
    w["rwkv_vecs"] = jnp.stack(rows)
    w2, a2 = p["rwkv_w2"][0], p["rwkv_a2"][0]
    zero = jnp.zeros_like(w2)
    w["rwkv_w2a"] = jnp.concatenate(
        [jnp.concatenate([w2, zero], axis=1), jnp.concatenate([zero, a2], axis=1)], axis=0).astype(BF16)
    w["rwkv_g2"] = p["rwkv_g2"][0].astype(BF16)
    sb = p["sb_w_in"][0].astype(BF16)
    d = sb.shape[0]
    w["sb_q"], w["sb_k"], w["sb_v"] = sb[:, :d], sb[:, d:2 * d], sb[:, 2 * d:]
    w["sb_out"] = p["sb_w_out"][0].astype(BF16)
    for n in ("mem_w_q", "mem_w_k", "mem_w_v", "mem_w_o", "ffn_w_in", "ffn_w_out"):
        w[n] = p[n].astype(BF16)
    return w


def _trunk(x, mem_k, mem_v, mem_stride, conv_buf, s_gdn, s_rwkv, shift, sb_past, p, w):
    b, t, d = x.shape
    m = b * t
    gp, rp, bg = _norm_proj(x.reshape(m, d), p["norm_mix"][0], [w["ev_g"], w["ev_r"], w["ev_bg"]])
    o_a, gdn_new, conv_new = _gdn(gp.reshape(b, t, -1), bg.reshape(b, t, LANES), conv_buf, s_gdn,
                                  p["gdn_conv_w"][0], w["gdn_hp"], p["gdn_norm"][0])
    o_b, rwkv_new, shift_new = _rwkv(rp.reshape(b, t, RWKV_IN), shift, _rwkv_state_to_pairs(s_rwkv),
                                     p["rwkv_mu"][0], w["rwkv_vecs"], w["rwkv_w2a"], w["rwkv_g2"])
    x2 = _proj_residual(x.reshape(m, d), [o_a.reshape(m, -1), o_b.reshape(m, -1)],
                        [w["ev_out_a"], w["ev_out_b"]])
    x3 = _mem_attn(x2.reshape(b, t, d), p["norm_mem"][0], w["mem_w_q"][0], w["mem_w_o"][0],
                   mem_k, mem_v, 0)
    x4 = _ffn(x3.reshape(m, d), p["norm_ffn"][0], w["ffn_w_in"][0], w["ffn_w_out"][0],
              p["norm_final"], False)
    q, k, v = _norm_proj(x4, p["norm_mix"][1], [w["sb_q"], w["sb_k"], w["sb_v"]])
    q, k, v = (a.reshape(b, t, d) for a in (q, k, v))
    if sb_past is None:
        y = _sb_prompt(q, k, v, p["sb_bias"][0])
    else:
        pool_k, pool_v, page_table = sb_past
        y = _sb_paged(q, k, v, pool_k, pool_v, page_table, 0, p["sb_bias"][0])
    x5 = _proj_residual(x4, [y.reshape(m, d)], [w["sb_out"]])
    x6 = _mem_attn(x5.reshape(b, t, d), p["norm_mem"][1], w["mem_w_q"][1], w["mem_w_o"][1],
                   mem_k, mem_v, mem_stride)
    y_out = _ffn(x6.reshape(m, d), p["norm_ffn"][1], w["ffn_w_in"][1], w["ffn_w_out"][1],
                 p["norm_final"], True).reshape(b, t, d)
    heads = (b, t, SB_HEADS, SB_HEAD_DIM)
    return (y_out, conv_new[None], gdn_new[None], _rwkv_state_from_pairs(rwkv_new)[None],
            shift_new.reshape(b, RWKV_IN)[None], k.reshape(heads)[None], v.reshape(heads)[None])


def kernel(x_prompt, x_sample, mem_prompt, state_gdn, state_gdn_conv, state_rwkv, state_rwkv_shift,
           cache_sb_k, cache_sb_v, cache_mem_k, cache_mem_v, page_table,
           norm_mix, norm_mem, norm_memtok, norm_ffn, norm_final, ev_w_in, ev_w_out,
           gdn_conv_w, gdn_a_log, gdn_dt_bias, gdn_norm, rwkv_mu, rwkv_w0, rwkv_w2, rwkv_a0, rwkv_a2, rwkv_g2,
           rwkv_k_k, rwkv_k_a, rwkv_r_k, rwkv_gn_g, rwkv_gn_b, sb_w_in, sb_w_out, sb_bias,
           mem_w_q, mem_w_k, mem_w_v, mem_w_o, ffn_w_in, ffn_w_out):
    p = dict(norm_mix=norm_mix, norm_mem=norm_mem, norm_memtok=norm_memtok, norm_ffn=norm_ffn,
             norm_final=norm_final, ev_w_in=ev_w_in, ev_w_out=ev_w_out, gdn_conv_w=gdn_conv_w,
             gdn_a_log=gdn_a_log, gdn_dt_bias=gdn_dt_bias, gdn_norm=gdn_norm, rwkv_mu=rwkv_mu,
             rwkv_w0=rwkv_w0, rwkv_w2=rwkv_w2, rwkv_a0=rwkv_a0, rwkv_a2=rwkv_a2, rwkv_g2=rwkv_g2,
             rwkv_k_k=rwkv_k_k, rwkv_k_a=rwkv_k_a, rwkv_r_k=rwkv_r_k, rwkv_gn_g=rwkv_gn_g,
             rwkv_gn_b=rwkv_gn_b, sb_w_in=sb_w_in, sb_w_out=sb_w_out, sb_bias=sb_bias,
             mem_w_q=mem_w_q, mem_w_k=mem_w_k, mem_w_v=mem_w_v, mem_w_o=mem_w_o,
             ffn_w_in=ffn_w_in, ffn_w_out=ffn_w_out)
    assert ev_w_in.shape[0] == 1 and sb_w_in.shape[0] == 1 and norm_mix.shape[0] == 2
    w = _prep_weights(p)
    bp, tp, d = x_prompt.shape
    bs = x_sample.shape[0]
    n_mem = mem_prompt.shape[1]

    mem2d = mem_prompt.reshape(bp * n_mem, d)
    mk_l, mv_l = [], []
    for layer in range(2):
        mk, mv = _norm_proj(mem2d, norm_memtok[layer], [w["mem_w_k"][layer], w["mem_w_v"][layer]])
        mk_l.append(mk.reshape(bp, n_mem, d))
        mv_l.append(mv.reshape(bp, n_mem, d))
    mem_k_p = jnp.stack(mk_l)
    mem_v_p = jnp.stack(mv_l)
    zeros = functools.partial(jnp.zeros, dtype=F32)
    out_p = _trunk(
        x_prompt, mem_k_p.reshape(2 * bp, n_mem, d), mem_v_p.reshape(2 * bp, n_mem, d), bp,
        zeros((bp, CONV_WIDTH - 1, GDN_CONV_CH)), zeros((bp, GDN_HEADS, GDN_HEAD_DIM, GDN_HEAD_DIM)),
        zeros((bp, 2 * RWKV_PAIRS, RWKV_HEAD_DIM, RWKV_HEAD_DIM)), zeros((bp, RWKV_IN)), None, p, w)

    n_pool = cache_sb_k.shape[1]
    pool_k = cache_sb_k.reshape(cache_sb_k.shape[0] * n_pool, PAGE_SIZE, d)
    pool_v = cache_sb_v.reshape(cache_sb_v.shape[0] * n_pool, PAGE_SIZE, d)
    out_s = _trunk(
        x_sample, cache_mem_k.reshape(2 * bs, n_mem, d), cache_mem_v.reshape(2 * bs, n_mem, d), bs,
        state_gdn_conv[0], state_gdn[0], state_rwkv[0], state_rwkv_shift[0],
        (pool_k, pool_v, page_table), p, w)

    y_p, conv_p, gdn_p, rwkv_p, shift_p, sbk_p, sbv_p = out_p
    y_s, conv_s, gdn_s, rwkv_s, shift_s, sbk_s, sbv_s = out_s
    mem_shape = (2, bp, n_mem, MEM_HEADS, d // MEM_HEADS)
    return (y_p, y_s, gdn_p, gdn_s, conv_p, conv_s, rwkv_p, rwkv_s, shift_p, shift_s,
            sbk_p, sbk_s, sbv_p, sbv_s, mem_k_p.reshape(mem_shape), mem_v_p.reshape(mem_shape))
```

```python
import functools
import math

import jax
import jax.numpy as jnp
from jax import lax
from jax.experimental import pallas as pl
from jax.experimental.pallas import tpu as pltpu

F32 = jnp.float32
BF16 = jnp.bfloat16

EPS = 1e-6
RWKV_GN_EPS = 64e-5
CONV_WIDTH = 4
GDN_HEADS = 4
GDN_HEAD_DIM = 128
GDN_WIDTH = GDN_HEADS * GDN_HEAD_DIM
GDN_CONV_CH = 3 * GDN_WIDTH
RWKV_HEAD_DIM = 64
RWKV_WIDTH = 512
RWKV_PAIRS = RWKV_WIDTH // 128
RWKV_IN = 1792
SB_HEADS = 16
SB_HEAD_DIM = 64
MEM_HEADS = 4
PAGE_SIZE = 128
LANES = 128
SUBLANES = 8
V7X_VMEM_LIMIT_BYTES = 56 * 1024 * 1024
PAGES_PER_STEP = 8


def _params(*sem):
    return pltpu.CompilerParams(dimension_semantics=sem, vmem_limit_bytes=V7X_VMEM_LIMIT_BYTES)


def _mm(a, b):
    return jnp.dot(a.astype(BF16), b.astype(BF16), preferred_element_type=F32)


def _mm_nt(a, b):
    return lax.dot_general(a.astype(BF16), b.astype(BF16), (((1,), (1,)), ((), ())),
                           preferred_element_type=F32)


def _mm_tn(a, b):
    return lax.dot_general(a.astype(BF16), b.astype(BF16), (((0,), (0,)), ((), ())),
                           preferred_element_type=F32)


def _hi_lo(x):
    hi = x.astype(BF16)
    lo = (x - hi.astype(F32)).astype(BF16)
    return hi, lo


def _mm3(a, b):
    ah, al = _hi_lo(a)
    bh, bl = _hi_lo(b)
    d = functools.partial(jnp.dot, preferred_element_type=F32)
    return d(ah, bh) + d(ah, bl) + d(al, bh)


INVERSE_PASSES = 1


def _mm_inv(a, b):
    return _mm3(a, b) if INVERSE_PASSES == 3 else _mm(a, b)


def _mm_exact_lhs(a_bf, x):
    xh, xl = _hi_lo(x)
    return (jnp.dot(a_bf, xh, preferred_element_type=F32)
            + jnp.dot(a_bf, xl, preferred_element_type=F32))


def _mm_exact_rhs(x, b_bf):
    xh, xl = _hi_lo(x)
    return (jnp.dot(xh, b_bf, preferred_element_type=F32)
            + jnp.dot(xl, b_bf, preferred_element_type=F32))


def _rms(x, g):
    return x * lax.rsqrt(jnp.mean(x * x, axis=-1, keepdims=True) + EPS) * g


def _softplus(x):
    return jnp.maximum(x, 0.0) + jnp.log1p(jnp.exp(-jnp.abs(x)))


def _iota2(shape, dim):
    return lax.broadcasted_iota(jnp.int32, shape, dim)


def _row_tile(m, want):
    t = min(m, want)
    while m % t:
        t //= 2
    return t


def _levels(c):
    return max(1, int(math.ceil(math.log2(c))))


def _norm_proj_kernel(x_ref, g_ref, *refs, n_w, outs):
    xn = _rms(x_ref[...], g_ref[...]).astype(BF16)
    prods = [jnp.dot(xn, w_ref[...], preferred_element_type=F32) for w_ref in refs[:n_w]]
    for (wi, dtype, scale), o_ref in zip(outs, refs[n_w:]):
        o_ref[...] = (prods[wi] if scale == 1.0 else prods[wi] * scale).astype(dtype)


def _norm_proj(x2d, g, ws, outs=None, tm=512):
    m, d = x2d.shape
    tm = _row_tile(m, tm)
    if outs is None:
        outs = [(i, F32, 1.0) for i in range(len(ws))]
    in_specs = [pl.BlockSpec((tm, d), lambda i: (i, 0)), pl.BlockSpec((1, d), lambda i: (0, 0))]
    in_specs += [pl.BlockSpec(w.shape, lambda i: (0, 0)) for w in ws]
    out_specs = [pl.BlockSpec((tm, ws[wi].shape[1]), lambda i: (i, 0)) for wi, _, _ in outs]
    out_shape = [jax.ShapeDtypeStruct((m, ws[wi].shape[1]), dt) for wi, dt, _ in outs]
    return pl.pallas_call(
        functools.partial(_norm_proj_kernel, n_w=len(ws), outs=tuple(outs)),
        grid=(m // tm,), in_specs=in_specs, out_specs=out_specs, out_shape=out_shape,
        compiler_params=_params("parallel"), name="norm_proj",
    )(x2d, g.reshape(1, d), *ws)


def _proj_res_kernel(res_ref, *refs, n_in):
    acc = res_ref[...]
    for a_ref, w_ref in zip(refs[:n_in], refs[n_in:2 * n_in]):
        acc = acc + jnp.dot(a_ref[...].astype(BF16), w_ref[...], preferred_element_type=F32)
    refs[2 * n_in][...] = acc


def _proj_residual(res2d, acts, ws, tm=512):
    m, d = res2d.shape
    tm = _row_tile(m, tm)
    n_in = len(acts)
    in_specs = [pl.BlockSpec((tm, d), lambda i: (i, 0))]
    in_specs += [pl.BlockSpec((tm, a.shape[1]), lambda i: (i, 0)) for a in acts]
    in_specs += [pl.BlockSpec(w.shape, lambda i: (0, 0)) for w in ws]
    return pl.pallas_call(
        functools.partial(_proj_res_kernel, n_in=n_in),
        grid=(m // tm,), in_specs=in_specs,
        out_specs=pl.BlockSpec((tm, d), lambda i: (i, 0)),
        out_shape=jax.ShapeDtypeStruct((m, d), F32),
        compiler_params=_params("parallel"), name="proj_residual",
    )(res2d, *acts, *ws)


def _ffn_kernel(x_ref, g_ref, wg_ref, wu_ref, wo_ref, gf_ref, o_ref, xn_ref, acc_ref, *, final_norm):
    j = pl.program_id(1)

    @pl.when(j == 0)
    def _():
        xn_ref[...] = _rms(x_ref[...], g_ref[...]).astype(BF16)
        acc_ref[...] = jnp.zeros_like(acc_ref)

    xn = xn_ref[...]
    gt = jnp.dot(xn, wg_ref[...], preferred_element_type=F32)
    ut = jnp.dot(xn, wu_ref[...], preferred_element_type=F32)
    act = (gt * jax.nn.sigmoid(gt) * ut).astype(BF16)
    acc_ref[...] += jnp.dot(act, wo_ref[...], preferred_element_type=F32)

    @pl.when(j == pl.num_programs(1) - 1)
    def _():
        y = x_ref[...] + acc_ref[...]
        if final_norm:
            y = _rms(y, gf_ref[...])
        o_ref[...] = y


def _ffn(x2d, g, w_in, w_out, g_final, final_norm, tm=512, n_chunks=2):
    m, d = x2d.shape
    dff = w_out.shape[0]
    tf = dff // n_chunks
    tm = _row_tile(m, tm)
    return pl.pallas_call(
        functools.partial(_ffn_kernel, final_norm=final_norm),
        grid=(m // tm, n_chunks),
        in_specs=[
            pl.BlockSpec((tm, d), lambda i, j: (i, 0)),
            pl.BlockSpec((1, d), lambda i, j: (0, 0)),
            pl.BlockSpec((d, tf), lambda i, j: (0, j)),
            pl.BlockSpec((d, tf), lambda i, j: (0, n_chunks + j)),
            pl.BlockSpec((tf, d), lambda i, j: (j, 0)),
            pl.BlockSpec((1, d), lambda i, j: (0, 0)),
        ],
        out_specs=pl.BlockSpec((tm, d), lambda i, j: (i, 0)),
        out_shape=jax.ShapeDtypeStruct((m, d), F32),
        scratch_shapes=[pltpu.VMEM((tm, d), BF16), pltpu.VMEM((tm, d), F32)],
        compiler_params=_params("parallel", "arbitrary"), name="ffn",
    )(x2d, g.reshape(1, d), w_in, w_in, w_out, g_final.reshape(1, d))


def _mem_attn_kernel(x_ref, g_ref, wq_ref, wo_ref, mk_ref, mv_ref, o_ref, *, scale, heads):
    x = x_ref[0]
    xn = _rms(x, g_ref[...]).astype(BF16)
    q = (jnp.dot(xn, wq_ref[...], preferred_element_type=F32) * scale).astype(BF16)
    mk = mk_ref[0].astype(BF16)
    mv = mv_ref[0].astype(BF16)
    hd = x.shape[-1] // heads
    outs = []
    for h in range(heads):
        sl = slice(h * hd, (h + 1) * hd)
        s = _mm_nt(q[:, sl], mk[:, sl])
        e = jnp.exp(s - jnp.max(s, axis=-1, keepdims=True))
        outs.append(_mm(e, mv[:, sl]) / jnp.sum(e, axis=-1, keepdims=True))
    o = jnp.concatenate(outs, axis=-1).astype(BF16)
    o_ref[0] = x + jnp.dot(o, wo_ref[...], preferred_element_type=F32)


def _mem_attn(x, g, wq, wo, mk, mv, mem_base, tm=512):
    b, t, d = x.shape
    tm = _row_tile(t, tm)
    n_mem = mk.shape[1]
    return pl.pallas_call(
        functools.partial(_mem_attn_kernel, scale=(d // MEM_HEADS) ** -0.5, heads=MEM_HEADS),
        grid=(b, t // tm),
        in_specs=[
            pl.BlockSpec((1, tm, d), lambda i, j: (i, j, 0)),
            pl.BlockSpec((1, d), lambda i, j: (0, 0)),
            pl.BlockSpec((d, d), lambda i, j: (0, 0)),
            pl.BlockSpec((d, d), lambda i, j: (0, 0)),
            pl.BlockSpec((1, n_mem, d), lambda i, j: (mem_base + i, 0, 0)),
            pl.BlockSpec((1, n_mem, d), lambda i, j: (mem_base + i, 0, 0)),
        ],
        out_specs=pl.BlockSpec((1, tm, d), lambda i, j: (i, j, 0)),
        out_shape=jax.ShapeDtypeStruct((b, t, d), F32),
        compiler_params=_params("parallel", "parallel"), name="mem_attn",
    )(x, g.reshape(1, d), wq, wo, mk, mv)


def _gdn_kernel(gp_ref, bg_ref, cb_ref, s0_ref, cw_ref, hp_ref, gn_ref, o_ref, so_ref, cv_ref,
                ext_ref, s_ref, *, chunk, levels):
    c = pl.program_id(1)
    C = chunk
    dk = GDN_HEAD_DIM
    tail = CONV_WIDTH - 1

    @pl.when(c == 0)
    def _():
        ext_ref[SUBLANES - tail:SUBLANES, :] = cb_ref[0]
        s_ref[...] = s0_ref[0]

    gp = gp_ref[0]
    u_in = gp[:, :GDN_CONV_CH]
    ext_ref[SUBLANES:SUBLANES + C, :] = u_in
    cw = cw_ref[...]
    conv = u_in * cw[tail:tail + 1]
    for j in range(tail):
        lo = SUBLANES - tail + j
        conv = conv + ext_ref[lo:lo + C, :] * cw[j:j + 1]
    new_tail = ext_ref[SUBLANES + C - tail:SUBLANES + C, :]
    ext_ref[SUBLANES - tail:SUBLANES, :] = new_tail
    cv_ref[0] = new_tail
    qkv = conv * jax.nn.sigmoid(conv)

    bg = bg_ref[0]
    lane = _iota2(bg.shape, 1)
    beta_all = jax.nn.sigmoid(bg)
    g_all = -jnp.exp(hp_ref[0:1, :]) * _softplus(bg + hp_ref[1:2, :])

    C2 = 2 * C
    r2 = _iota2((C2, C2), 0)
    c2 = _iota2((C2, C2), 1)
    same = (r2 >= C) == (c2 >= C)
    lower = same & (r2 >= c2)
    strict = same & (r2 > c2)
    tri = lower.astype(BF16)
    block_ones = same.astype(BF16)
    eye = (r2 == c2).astype(F32)
    first = _iota2((C2, dk), 0) < C

    def head(h):
        qh = qkv[:, h * dk:(h + 1) * dk]
        kh = qkv[:, GDN_WIDTH + h * dk:GDN_WIDTH + (h + 1) * dk]
        vh = qkv[:, 2 * GDN_WIDTH + h * dk:2 * GDN_WIDTH + (h + 1) * dk]
        qn = qh * lax.rsqrt(jnp.sum(qh * qh, axis=-1, keepdims=True) + EPS) * (dk ** -0.5)
        kn = kh * lax.rsqrt(jnp.sum(kh * kh, axis=-1, keepdims=True) + EPS)
        beta = jnp.sum(jnp.where(lane == h, beta_all, 0.0), axis=-1, keepdims=True)
        g = jnp.sum(jnp.where(lane == GDN_HEADS + h, g_all, 0.0), axis=-1, keepdims=True)
        return qn, kn, vh, beta, g

    groups = []
    for gi in range(GDN_HEADS // 2):
        h0, h1 = head(2 * gi), head(2 * gi + 1)
        qs, ks, vs, beta, g = (jnp.concatenate([a, b], axis=0) for a, b in zip(h0, h1))
        gcum = _mm_exact_lhs(tri, jnp.broadcast_to(g, (C2, dk)))
        g_cc = jnp.broadcast_to(g, (C2, C2))
        gcum_i = _mm_exact_lhs(tri, g_cc)
        gcum_j = _mm_exact_lhs(block_ones, jnp.where(same & (r2 <= c2), g_cc, 0.0))
        decay = jnp.where(lower, jnp.exp(jnp.minimum(gcum_i - gcum_j, 0.0)), 0.0)
        gam = jnp.exp(gcum)
        glast = jnp.where(first, gcum[C - 1:C, :], gcum[C2 - 1:C2, :])
        x = jnp.where(strict, -(beta * _mm_nt(ks, ks) * decay), 0.0)
        groups.append(dict(qs=qs, ks=ks, vs=vs, beta=beta, gam=gam, glast=glast, gcum=gcum, x=x,
                           qkd=_mm_nt(qs, ks) * decay))

    invs = [eye + gr["x"] for gr in groups]
    pws = [gr["x"] for gr in groups]
    for _ in range(levels - 1):
        pws = [_mm_inv(pw, pw) for pw in pws]
        invs = [inv + _mm_inv(inv, pw) for inv, pw in zip(invs, pws)]

    heads = [(gi, j, slice(j * C, (j + 1) * C)) for gi in range(len(groups)) for j in range(2)]
    w_mats = [_mm(inv, gr["beta"] * gr["gam"] * gr["ks"]) for gr, inv in zip(groups, invs)]
    u0s = [_mm(inv, gr["beta"] * gr["vs"]) for gr, inv in zip(groups, invs)]
    kdecs = [gr["ks"] * jnp.exp(gr["glast"] - gr["gcum"]) for gr in groups]
    states = [s_ref[2 * gi + j] for gi, j, _ in heads]
    ws_s = [_mm(w_mats[gi][rows], s) for (gi, _, rows), s in zip(heads, states)]
    qs_s = [_mm(groups[gi]["qs"][rows], s) for (gi, _, rows), s in zip(heads, states)]
    us = [u0s[gi][rows] - ws for (gi, _, rows), ws in zip(heads, ws_s)]
    u_g = [jnp.concatenate(us[2 * gi:2 * gi + 2], axis=0) for gi in range(len(groups))]
    o_g = [gr["gam"] * jnp.concatenate(qs_s[2 * gi:2 * gi + 2], axis=0) + _mm(gr["qkd"], u_g[gi])
           for gi, gr in enumerate(groups)]
    upds = [_mm_tn(kdecs[gi][rows], u) for (gi, _, rows), u in zip(heads, us)]
    for (gi, j, rows), s, upd in zip(heads, states, upds):
        h = 2 * gi + j
        s_ref[h] = jnp.exp(groups[gi]["glast"][j * C:j * C + 1, :]) * s + upd
        zh = gp[:, GDN_CONV_CH + h * dk:GDN_CONV_CH + (h + 1) * dk]
        o_ref[0, :, h * dk:(h + 1) * dk] = _rms(o_g[gi][rows], gn_ref[...]) * (zh * jax.nn.sigmoid(zh))

    @pl.when(c == pl.num_programs(1) - 1)
    def _():
        so_ref[0] = s_ref[...]


def _gdn(gp, bg, conv_buf, s0, conv_w, head_params, gnorm):
    b, t, _ = gp.shape
    chunk = min(64, t)
    assert t % chunk == 0 and chunk % SUBLANES == 0 and chunk >= CONV_WIDTH - 1
    return pl.pallas_call(
        functools.partial(_gdn_kernel, chunk=chunk, levels=_levels(chunk)),
        grid=(b, t // chunk),
        in_specs=[
            pl.BlockSpec((1, chunk, gp.shape[2]), lambda i, j: (i, j, 0)),
            pl.BlockSpec((1, chunk, LANES), lambda i, j: (i, j, 0)),
            pl.BlockSpec((1, CONV_WIDTH - 1, GDN_CONV_CH), lambda i, j: (i, 0, 0)),
            pl.BlockSpec((1, GDN_HEADS, GDN_HEAD_DIM, GDN_HEAD_DIM), lambda i, j: (i, 0, 0, 0)),
            pl.BlockSpec((CONV_WIDTH, GDN_CONV_CH), lambda i, j: (0, 0)),
            pl.BlockSpec((SUBLANES, LANES), lambda i, j: (0, 0)),
            pl.BlockSpec((1, GDN_HEAD_DIM), lambda i, j: (0, 0)),
        ],
        out_specs=[
            pl.BlockSpec((1, chunk, GDN_WIDTH), lambda i, j: (i, j, 0)),
            pl.BlockSpec((1, GDN_HEADS, GDN_HEAD_DIM, GDN_HEAD_DIM), lambda i, j: (i, 0, 0, 0)),
            pl.BlockSpec((1, CONV_WIDTH - 1, GDN_CONV_CH), lambda i, j: (i, 0, 0)),
        ],
        out_shape=[
            jax.ShapeDtypeStruct((b, t, GDN_WIDTH), F32),
            jax.ShapeDtypeStruct((b, GDN_HEADS, GDN_HEAD_DIM, GDN_HEAD_DIM), F32),
            jax.ShapeDtypeStruct((b, CONV_WIDTH - 1, GDN_CONV_CH), F32),
        ],
        scratch_shapes=[
            pltpu.VMEM((chunk + SUBLANES, GDN_CONV_CH), F32),
            pltpu.VMEM((GDN_HEADS, GDN_HEAD_DIM, GDN_HEAD_DIM), F32),
        ],
        compiler_params=_params("parallel", "arbitrary"), name="gdn",
    )(gp, bg, conv_buf, s0, conv_w, head_params, gnorm.reshape(1, GDN_HEAD_DIM))


def _rwkv_kernel(rp_ref, sh_ref, s0_ref, mu_ref, vec_ref, w2a_ref, g2_ref, o_ref, so_ref, sho_ref,
                 last_ref, s_ref, *, chunk, levels):
    c = pl.program_id(1)
    C = chunk
    W = RWKV_WIDTH
    hd = RWKV_HEAD_DIM

    @pl.when(c == 0)
    def _():
        last_ref[...] = sh_ref[0]
        s_ref[...] = s0_ref[0]

    rp = rp_ref[0]
    rowi = _iota2(rp.shape, 0)
    prev = jnp.where(rowi == 0, last_ref[...], pltpu.roll(rp, 1, 0))
    xr = rp + (prev - rp) * mu_ref[...]
    last_row = rp[C - 1:C, :]
    last_ref[...] = last_row
    sho_ref[0] = last_row

    vec = vec_ref[...]
    r_all = xr[:, :W]
    kr = xr[:, W:2 * W]
    v_all = xr[:, 2 * W:3 * W]
    pwa = xr[:, 3 * W:3 * W + LANES]
    pg = xr[:, 3 * W + LANES:]
    lane = _iota2((C, LANES), 1)
    m0 = lane < hd
    wa = _mm(jnp.where(m0, jnp.tanh(pwa), pwa), w2a_ref[...])
    w_raw = vec[0:1] + wa[:, :W]
    a_all = jax.nn.sigmoid(vec[1:2] + wa[:, W:])
    lw_all = -jnp.exp(-_softplus(-w_raw) - 0.5)
    gate = _mm(jax.nn.sigmoid(pg), g2_ref[...])
    kkp = kr * vec[2:3]
    k2_all = kr * (1.0 + (a_all - 1.0) * vec[3:4])

    r128 = _iota2((LANES, LANES), 0)
    c128 = _iota2((LANES, LANES), 1)
    bd_mask = (r128 >= hd) == (c128 >= hd)
    bd = bd_mask.astype(BF16)
    rc = _iota2((C, C), 0)
    cc = _iota2((C, C), 1)
    tri = (rc >= cc).astype(BF16)
    r2 = _iota2((2 * C, 2 * C), 0)
    c2 = _iota2((2 * C, 2 * C), 1)
    same = (r2 >= C) == (c2 >= C)
    strict = same & (r2 > c2)
    incl = same & (r2 >= c2)
    eye2 = (r2 == c2).astype(F32)

    def stack(x):
        return jnp.concatenate([x, x], axis=0)

    def stack_masked(x):
        return jnp.concatenate([jnp.where(m0, x, 0.0), jnp.where(m0, 0.0, x)], axis=0)

    def sel(z):
        return jnp.where(m0, z[:C], z[C:])

    pairs = []
    for p in range(RWKV_PAIRS):
        sl = slice(p * LANES, (p + 1) * LANES)
        r = r_all[:, sl]
        k2 = k2_all[:, sl]
        v = v_all[:, sl]
        lw = lw_all[:, sl]
        kk0 = kkp[:, sl]
        kk = kk0 * lax.rsqrt(_mm_exact_rhs(kk0 * kk0, bd) + EPS)

        gcum = _mm_exact_lhs(tri, lw)
        e_in = jnp.exp(gcum)
        e_inv = jnp.exp(-gcum)
        abar = -kk * jnp.exp(gcum - lw)
        bbar = kk * a_all[:, sl] * e_inv
        kbar = k2 * e_inv
        rbar = r * e_in

        a_s = stack_masked(abar)
        r_s = stack_masked(rbar)
        b_c = stack(bbar)
        k_c = stack(kbar)
        pairs.append(dict(
            sl=sl, r=r, k2=k2, v=v, abar=abar, bbar=bbar, kbar=kbar, rbar=rbar, e_last=e_in[C - 1:C, :],
            l_ab=jnp.where(strict, _mm_nt(a_s, b_c), 0.0), l_ak=jnp.where(strict, _mm_nt(a_s, k_c), 0.0),
            m_rb=jnp.where(incl, _mm_nt(r_s, b_c), 0.0), m_rk=jnp.where(incl, _mm_nt(r_s, k_c), 0.0)))

    invs = [eye2 + pr["l_ab"] for pr in pairs]
    pws = [pr["l_ab"] for pr in pairs]
    for _ in range(levels - 1):
        pws = [_mm_inv(pw, pw) for pw in pws]
        invs = [inv + _mm_inv(inv, pw) for inv, pw in zip(invs, pws)]

    inv_n = 1.0 / hd
    v_cs = [stack(pr["v"]) for pr in pairs]
    lakv = [_mm(pr["l_ak"], v_c) for pr, v_c in zip(pairs, v_cs)]
    mrkv = [_mm(pr["m_rk"], v_c) for pr, v_c in zip(pairs, v_cs)]
    bonus = [_mm_exact_rhs(pr["r"] * pr["k2"] * vec[4:5, pr["sl"]], bd) * pr["v"] for pr in pairs]
    states = [s_ref[p] for p in range(RWKV_PAIRS)]
    ars = [_mm_nt(jnp.concatenate([pr["abar"], pr["rbar"]], axis=0), s) for pr, s in zip(pairs, states)]
    rhss = [ar[:C] + sel(x) for ar, x in zip(ars, lakv)]
    us = [sel(_mm(inv, stack(rhs))) for inv, rhs in zip(invs, rhss)]
    ys = [ar[C:] + sel(_mm(pr["m_rb"], stack(u)) + x) for ar, pr, u, x in zip(ars, pairs, us, mrkv)]
    upds = [_mm_tn(jnp.concatenate([u, pr["v"]], axis=0), jnp.concatenate([pr["bbar"], pr["kbar"]], axis=0))
            for u, pr in zip(us, pairs)]
    for p, (pr, s, upd) in enumerate(zip(pairs, states, upds)):
        s_ref[p] = (s + jnp.where(bd_mask, upd, 0.0)) * pr["e_last"]
    ycs = [y - _mm_exact_rhs(y, bd) * inv_n for y in ys]
    vrs = [_mm_exact_rhs(yc * yc, bd) * inv_n for yc in ycs]
    for pr, yc, var, bon in zip(pairs, ycs, vrs, bonus):
        sl = pr["sl"]
        yn = yc * lax.rsqrt(var + RWKV_GN_EPS) * vec[5:6, sl] + vec[6:7, sl]
        o_ref[0, :, sl] = (yn + bon) * gate[:, sl]

    @pl.when(c == pl.num_programs(1) - 1)
    def _():
        so_ref[0] = s_ref[...]


def _rwkv(rp, shift, s0_pairs, mu, vecs, w2a, g2):
    b, t, _ = rp.shape
    chunk = min(64, t)
    assert t % chunk == 0 and chunk % SUBLANES == 0
    return pl.pallas_call(
        functools.partial(_rwkv_kernel, chunk=chunk, levels=_levels(chunk)),
        grid=(b, t // chunk),
        in_specs=[
            pl.BlockSpec((1, chunk, RWKV_IN), lambda i, j: (i, j, 0)),
            pl.BlockSpec((1, 1, RWKV_IN), lambda i, j: (i, 0, 0)),
            pl.BlockSpec((1, RWKV_PAIRS, LANES, LANES), lambda i, j: (i, 0, 0, 0)),
            pl.BlockSpec((1, RWKV_IN), lambda i, j: (0, 0)),
            pl.BlockSpec((SUBLANES, RWKV_WIDTH), lambda i, j: (0, 0)),
            pl.BlockSpec(w2a.shape, lambda i, j: (0, 0)),
            pl.BlockSpec(g2.shape, lambda i, j: (0, 0)),
        ],
        out_specs=[
            pl.BlockSpec((1, chunk, RWKV_WIDTH), lambda i, j: (i, j, 0)),
            pl.BlockSpec((1, RWKV_PAIRS, LANES, LANES), lambda i, j: (i, 0, 0, 0)),
            pl.BlockSpec((1, 1, RWKV_IN), lambda i, j: (i, 0, 0)),
        ],
        out_shape=[
            jax.ShapeDtypeStruct((b, t, RWKV_WIDTH), F32),
            jax.ShapeDtypeStruct((b, RWKV_PAIRS, LANES, LANES), F32),
            jax.ShapeDtypeStruct((b, 1, RWKV_IN), F32),
        ],
        scratch_shapes=[
            pltpu.VMEM((1, RWKV_IN), F32),
            pltpu.VMEM((RWKV_PAIRS, LANES, LANES), F32),
        ],
        compiler_params=_params("parallel", "arbitrary"), name="rwkv7",
    )(rp, shift.reshape(b, 1, RWKV_IN), s0_pairs, mu.reshape(1, RWKV_IN), vecs, w2a, g2)


def _sb_tile(z, r_later, u_bf, mask):
    (att,), (r_new,) = _sb_tiles([z], [r_later], u_bf, mask, chained=False)
    return att, r_new


def _sb_tiles(zs, r_in, u_bf, mask, chained):
    sps = [jnp.maximum(z, 0.0) + jnp.log(1.0 + jnp.exp(-jnp.abs(z))) for z in zs]
    if mask is not None:
        sps = [jnp.where(mask, sp, 0.0) for sp in sps]
    splits = [_hi_lo(sp) for sp in sps]
    d = functools.partial(jnp.dot, preferred_element_type=F32)
    cums = [d(hi, u_bf) + d(lo, u_bf) for hi, lo in splits]
    sums = [jnp.sum(sp, axis=-1, keepdims=True) for sp in sps]
    if chained:
        rs = [r_in[0]]
        for s in sums:
            rs.append(rs[-1] + s)
        r_tiles, r_out = rs[:-1], rs[-1:]
    else:
        r_tiles = r_in
        r_out = [r + s for r, s in zip(r_in, sums)]
    atts = [jnp.exp(z - cum - r) for z, cum, r in zip(zs, cums, r_tiles)]
    if mask is not None:
        atts = [jnp.where(mask, att, 0.0) for att in atts]
    return atts, r_out


def _sb_prompt_kernel(bias_ref, q_ref, k_ref, v_ref, o_ref, acc_ref, *, blk, kt, pg):
    g = pl.program_id(1)
    qi = pl.program_id(2)
    hd = SB_HEAD_DIM
    m0 = _iota2((blk, LANES), 1) < hd
    u_bf = (_iota2((kt, kt), 0) >= _iota2((kt, kt), 1)).astype(BF16)
    ktop = (qi * blk) // kt
    causal = (ktop * kt + _iota2((blk, kt), 1)) < (qi * blk + _iota2((blk, kt), 0))
    qs, bias = [], []
    for pp in range(pg):
        qp = q_ref[0, :, pp * LANES:(pp + 1) * LANES]
        zero = jnp.zeros_like(qp)
        qs.append((jnp.where(m0, qp, zero), jnp.where(m0, zero, qp)))
        head = 2 * (g * pg + pp)
        bias.append((bias_ref[head], bias_ref[head + 1]))
    acc_ref[...] = jnp.zeros_like(acc_ref)

    def tile(kj, r_later, mask):
        start = pl.multiple_of(kj * kt, kt)
        kbs = [k_ref[0, pl.ds(start, kt), pp * LANES:(pp + 1) * LANES] for pp in range(pg)]
        vbs = [v_ref[0, pl.ds(start, kt), pp * LANES:(pp + 1) * LANES] for pp in range(pg)]
        zs = [lax.dot_general(qs[pp][h], kbs[pp], (((1,), (1,)), ((), ())),
                              preferred_element_type=F32) + bias[pp][h]
              for pp in range(pg) for h in range(2)]
        atts, r_new = _sb_tiles(zs, list(r_later), u_bf, mask, chained=False)
        pvs = [jnp.dot(att.astype(BF16), vbs[i // 2], preferred_element_type=F32)
               for i, att in enumerate(atts)]
        for pp in range(pg):
            acc_ref[pp] += jnp.where(m0, pvs[2 * pp], pvs[2 * pp + 1])
        return tuple(r_new)

    r_later = tuple(jnp.zeros((blk, 1), F32) for _ in range(2 * pg))
    r_later = tile(ktop, r_later, causal)
    lax.fori_loop(0, ktop, lambda i, r: tile(ktop - 1 - i, r, None), r_later)
    for pp in range(pg):
        o_ref[0, :, pp * LANES:(pp + 1) * LANES] = acc_ref[pp]


def _sb_prompt(q, k, v, bias, pg=4):
    b, t, d = q.shape
    blk = 128 if t % 128 == 0 else t
    kt = 256 if t % 256 == 0 else blk
    assert blk % (2 * SUBLANES) == 0 and kt % blk == 0 and (d // LANES) % pg == 0
    w = pg * LANES
    return pl.pallas_call(
        functools.partial(_sb_prompt_kernel, blk=blk, kt=kt, pg=pg),
        grid=(b, d // w, t // blk),
        in_specs=[
            pl.BlockSpec(memory_space=pltpu.SMEM),
            pl.BlockSpec((1, blk, w), lambda i, p, j: (i, j, p)),
            pl.BlockSpec((1, t, w), lambda i, p, j: (i, 0, p)),
            pl.BlockSpec((1, t, w), lambda i, p, j: (i, 0, p)),
        ],
        out_specs=pl.BlockSpec((1, blk, w), lambda i, p, j: (i, j, p)),
        out_shape=jax.ShapeDtypeStruct((b, t, d), F32),
        scratch_shapes=[pltpu.VMEM((pg, blk, LANES), F32)],
        compiler_params=_params("parallel", "parallel", "arbitrary"), name="sb_prompt",
    )(bias, q, k, v)


def _sb_paged_kernel(pt_ref, bias_ref, q_ref, kn_ref, vn_ref, *refs, scale, pps, tq):
    k_refs = refs[:pps]
    v_refs = refs[pps:2 * pps]
    o_ref, qx_ref, acc_ref, r_ref = refs[2 * pps:]
    s = pl.program_id(1)
    hd = SB_HEAD_DIM
    rows = SB_HEADS * tq
    d = SB_HEADS * hd

    @pl.when(s == 0)
    def _():
        q = q_ref[0] * scale
        qt = jnp.concatenate([q] * SB_HEADS, axis=0)
        rh = _iota2((rows, d), 0) // tq
        lh = _iota2((rows, d), 1) // hd
        qx_ref[...] = jnp.where(rh == lh, qt, 0.0).astype(BF16)
        z = _mm_nt(qx_ref[...], kn_ref[0]) + bias_ref[:, :tq]
        rr = _iota2((rows, tq), 0)
        cc = _iota2((rows, tq), 1)
        mask = cc < (rr & (tq - 1))
        ur = _iota2((tq, tq), 0)
        uc = _iota2((tq, tq), 1)
        att, r_later = _sb_tile(z, jnp.zeros((rows, 1), F32), (ur >= uc).astype(BF16), mask)
        acc_ref[...] = _mm(att, vn_ref[0])
        r_ref[...] = jnp.broadcast_to(r_later, r_ref.shape)

    ur = _iota2((PAGE_SIZE, PAGE_SIZE), 0)
    uc = _iota2((PAGE_SIZE, PAGE_SIZE), 1)
    u_bf = (ur >= uc).astype(BF16)
    qx = qx_ref[...]
    zs = [_mm_nt(qx, k_refs[j][0]) + bias_ref[...] for j in range(pps)]
    atts, (r_later,) = _sb_tiles(zs, [r_ref[...]], u_bf, None, chained=True)
    acc = acc_ref[...]
    for j in range(pps):
        acc = acc + _mm(atts[j], v_refs[j][0])
    acc_ref[...] = acc
    r_ref[...] = r_later

    @pl.when(s == pl.num_programs(1) - 1)
    def _():
        lh = _iota2((tq, d), 1) // hd
        out = jnp.zeros((tq, d), F32)
        for h in range(SB_HEADS):
            out = out + jnp.where(lh == h, acc[h * tq:(h + 1) * tq, :], 0.0)
        o_ref[0] = out


def _sb_paged(q, k_new, v_new, pool_k, pool_v, page_table, pool_base, bias):
    b, tq, d = q.shape
    n_pages = page_table.shape[1]
    pps = PAGES_PER_STEP
    while n_pages % pps:
        pps //= 2
    assert tq & (tq - 1) == 0 and (SB_HEADS * tq) % SUBLANES == 0
    rows = SB_HEADS * tq
    bias_rows = jnp.broadcast_to(jnp.repeat(bias, tq)[:, None], (rows, PAGE_SIZE))

    def page_map(j):
        return lambda i, s, pt: (pool_base + pt[i, n_pages - 1 - (s * pps + j)], 0, 0)

    page_specs = [pl.BlockSpec((1, PAGE_SIZE, d), page_map(j)) for j in range(pps)]
    row_spec = pl.BlockSpec((1, tq, d), lambda i, s, pt: (i, 0, 0))
    grid_spec = pltpu.PrefetchScalarGridSpec(
        num_scalar_prefetch=1,
        grid=(b, n_pages // pps),
        in_specs=[pl.BlockSpec((rows, PAGE_SIZE), lambda i, s, pt: (0, 0)), row_spec, row_spec, row_spec]
        + page_specs + page_specs,
        out_specs=row_spec,
        scratch_shapes=[pltpu.VMEM((rows, d), BF16), pltpu.VMEM((rows, d), F32),
                        pltpu.VMEM((rows, PAGE_SIZE), F32)],
    )
    return pl.pallas_call(
        functools.partial(_sb_paged_kernel, scale=SB_HEAD_DIM ** -0.5, pps=pps, tq=tq),
        grid_spec=grid_spec,
        out_shape=jax.ShapeDtypeStruct((b, tq, d), F32),
        compiler_params=_params("parallel", "arbitrary"), name="sb_paged",
    )(page_table, bias_rows, q, k_new, v_new, *([pool_k] * pps), *([pool_v] * pps))


def _rwkv_state_to_pairs(s):
    b = s.shape[0]
    s = s.reshape(b, RWKV_PAIRS, 2, RWKV_HEAD_DIM, RWKV_HEAD_DIM)
    z = jnp.zeros_like(s[:, :, 0])
    top = jnp.concatenate([s[:, :, 0], z], axis=-1)
    bot = jnp.concatenate([z, s[:, :, 1]], axis=-1)
    return jnp.concatenate([top, bot], axis=-2)


def _rwkv_state_from_pairs(sp):
    b = sp.shape[0]
    hd = RWKV_HEAD_DIM
    return jnp.stack([sp[:, :, :hd, :hd], sp[:, :, hd:, hd:]], axis=2).reshape(b, 2 * RWKV_PAIRS, hd, hd)


def _prep_weights(p):
    w = {}
    ev = p["ev_w_in"][0]
    gdn_in = 4 * GDN_WIDTH
    w["ev_g"] = ev[:, :gdn_in].astype(BF16)
    w["ev_r"] = ev[:, gdn_in + 2 * GDN_HEADS:].astype(BF16)
    w["ev_bg"] = jnp.pad(ev[:, gdn_in:gdn_in + 2 * GDN_HEADS],
                         ((0, 0), (0, LANES - 2 * GDN_HEADS))).astype(BF16)
    w_out = p["ev_w_out"][0].astype(BF16)
    w["ev_out_a"], w["ev_out_b"] = w_out[:GDN_WIDTH], w_out[GDN_WIDTH:]
    hp = jnp.zeros((SUBLANES, LANES), F32)
    hp = hp.at[0, GDN_HEADS:2 * GDN_HEADS].set(p["gdn_a_log"][0])
    hp = hp.at[1, GDN_HEADS:2 * GDN_HEADS].set(p["gdn_dt_bias"][0])
    w["gdn_hp"] = hp
    names = ("rwkv_w0", "rwkv_a0", "rwkv_k_k", "rwkv_k_a", "rwkv_r_k", "rwkv_gn_g", "rwkv_gn_b")
    rows = [p[n][0] for n in names] + [jnp.zeros((RWKV_WIDTH,), F32)]
    w["rwkv_vecs"] = jnp.stack(rows)
    w2, a2 = p["rwkv_w2"][0], p["rwkv_a2"][0]
    zero = jnp.zeros_like(w2)
    w["rwkv_w2a"] = jnp.concatenate(
        [jnp.concatenate([w2, zero], axis=1), jnp.concatenate([zero, a2], axis=1)], axis=0).astype(BF16)
    w["rwkv_g2"] = p["rwkv_g2"][0].astype(BF16)
    sb = p["sb_w_in"][0].astype(BF16)
    d = sb.shape[0]
    w["sb_q"], w["sb_k"], w["sb_v"] = sb[:, :d], sb[:, d:2 * d], sb[:, 2 * d:]
    w["sb_out"] = p["sb_w_out"][0].astype(BF16)
    for n in ("mem_w_q", "mem_w_k", "mem_w_v", "mem_w_o", "ffn_w_in", "ffn_w_out"):
        w[n] = p[n].astype(BF16)
    return w


def _trunk(x, mem_k, mem_v, mem_stride, conv_buf, s_gdn, s_rwkv, shift, sb_past, p, w):
    b, t, d = x.shape
    m = b * t
    gp, rp, bg = _norm_proj(x.reshape(m, d), p["norm_mix"][0], [w["ev_g"], w["ev_r"], w["ev_bg"]])
    o_a, gdn_new, conv_new = _gdn(gp.reshape(b, t, -1), bg.reshape(b, t, LANES), conv_buf, s_gdn,
                                  p["gdn_conv_w"][0], w["gdn_hp"], p["gdn_norm"][0])
    o_b, rwkv_new, shift_new = _rwkv(rp.reshape(b, t, RWKV_IN), shift, _rwkv_state_to_pairs(s_rwkv),
                                     p["rwkv_mu"][0], w["rwkv_vecs"], w["rwkv_w2a"], w["rwkv_g2"])
    x2 = _proj_residual(x.reshape(m, d), [o_a.reshape(m, -1), o_b.reshape(m, -1)],
                        [w["ev_out_a"], w["ev_out_b"]])
    x3 = _mem_attn(x2.reshape(b, t, d), p["norm_mem"][0], w["mem_w_q"][0], w["mem_w_o"][0],
                   mem_k, mem_v, 0)
    x4 = _ffn(x3.reshape(m, d), p["norm_ffn"][0], w["ffn_w_in"][0], w["ffn_w_out"][0],
              p["norm_final"], False)
    sb_w = [w["sb_q"], w["sb_k"], w["sb_v"]]
    if sb_past is None:
        outs = [(0, BF16, SB_HEAD_DIM ** -0.5), (1, F32, 1.0), (2, F32, 1.0), (1, BF16, 1.0), (2, BF16, 1.0)]
        q_bf, k, v, k_bf, v_bf = (a.reshape(b, t, d) for a in _norm_proj(x4, p["norm_mix"][1], sb_w, outs))
        y = _sb_prompt(q_bf, k_bf, v_bf, p["sb_bias"][0])
    else:
        q, k, v = (a.reshape(b, t, d) for a in _norm_proj(x4, p["norm_mix"][1], sb_w))
        pool_k, pool_v, page_table = sb_past
        y = _sb_paged(q, k, v, pool_k, pool_v, page_table, 0, p["sb_bias"][0])
    x5 = _proj_residual(x4, [y.reshape(m, d)], [w["sb_out"]])
    x6 = _mem_attn(x5.reshape(b, t, d), p["norm_mem"][1], w["mem_w_q"][1], w["mem_w_o"][1],
                   mem_k, mem_v, mem_stride)
    y_out = _ffn(x6.reshape(m, d), p["norm_ffn"][1], w["ffn_w_in"][1], w["ffn_w_out"][1],
                 p["norm_final"], True).reshape(b, t, d)
    heads = (b, t, SB_HEADS, SB_HEAD_DIM)
    return (y_out, conv_new[None], gdn_new[None], _rwkv_state_from_pairs(rwkv_new)[None],
            shift_new.reshape(b, RWKV_IN)[None], k.reshape(heads)[None], v.reshape(heads)[None])


def kernel(x_prompt, x_sample, mem_prompt, state_gdn, state_gdn_conv, state_rwkv, state_rwkv_shift,
           cache_sb_k, cache_sb_v, cache_mem_k, cache_mem_v, page_table,
           norm_mix, norm_mem, norm_memtok, norm_ffn, norm_final, ev_w_in, ev_w_out,
           gdn_conv_w, gdn_a_log, gdn_dt_bias, gdn_norm, rwkv_mu, rwkv_w0, rwkv_w2, rwkv_a0, rwkv_a2, rwkv_g2,
           rwkv_k_k, rwkv_k_a, rwkv_r_k, rwkv_gn_g, rwkv_gn_b, sb_w_in, sb_w_out, sb_bias,
           mem_w_q, mem_w_k, mem_w_v, mem_w_o, ffn_w_in, ffn_w_out):
    p = dict(norm_mix=norm_mix, norm_mem=norm_mem, norm_memtok=norm_memtok, norm_ffn=norm_ffn,
             norm_final=norm_final, ev_w_in=ev_w_in, ev_w_out=ev_w_out, gdn_conv_w=gdn_conv_w,
             gdn_a_log=gdn_a_log, gdn_dt_bias=gdn_dt_bias, gdn_norm=gdn_norm, rwkv_mu=rwkv_mu,
             rwkv_w0=rwkv_w0, rwkv_w2=rwkv_w2, rwkv_a0=rwkv_a0, rwkv_a2=rwkv_a2, rwkv_g2=rwkv_g2,
             rwkv_k_k=rwkv_k_k, rwkv_k_a=rwkv_k_a, rwkv_r_k=rwkv_r_k, rwkv_gn_g=rwkv_gn_g,
             rwkv_gn_b=rwkv_gn_b, sb_w_in=sb_w_in, sb_w_out=sb_w_out, sb_bias=sb_bias,
             mem_w_q=mem_w_q, mem_w_k=mem_w_k, mem_w_v=mem_w_v, mem_w_o=mem_w_o,
             ffn_w_in=ffn_w_in, ffn_w_out=ffn_w_out)
    assert ev_w_in.shape[0] == 1 and sb_w_in.shape[0] == 1 and norm_mix.shape[0] == 2
    w = _prep_weights(p)
    bp, tp, d = x_prompt.shape
    bs = x_sample.shape[0]
    n_mem = mem_prompt.shape[1]

    mem2d = mem_prompt.reshape(bp * n_mem, d)
    mk_l, mv_l = [], []
    for layer in range(2):
        mk, mv = _norm_proj(mem2d, norm_memtok[layer], [w["mem_w_k"][layer], w["mem_w_v"][layer]])
        mk_l.append(mk.reshape(bp, n_mem, d))
        mv_l.append(mv.reshape(bp, n_mem, d))
    mem_k_p = jnp.stack(mk_l)
    mem_v_p = jnp.stack(mv_l)
    zeros = functools.partial(jnp.zeros, dtype=F32)
    out_p = _trunk(
        x_prompt, mem_k_p.reshape(2 * bp, n_mem, d), mem_v_p.reshape(2 * bp, n_mem, d), bp,
        zeros((bp, CONV_WIDTH - 1, GDN_CONV_CH)), zeros((bp, GDN_HEADS, GDN_HEAD_DIM, GDN_HEAD_DIM)),
        zeros((bp, 2 * RWKV_PAIRS, RWKV_HEAD_DIM, RWKV_HEAD_DIM)), zeros((bp, RWKV_IN)), None, p, w)

    n_pool = cache_sb_k.shape[1]
    pool_k = cache_sb_k.reshape(cache_sb_k.shape[0] * n_pool, PAGE_SIZE, d)
    pool_v = cache_sb_v.reshape(cache_sb_v.shape[0] * n_pool, PAGE_SIZE, d)
    out_s = _trunk(
        x_sample, cache_mem_k.reshape(2 * bs, n_mem, d), cache_mem_v.reshape(2 * bs, n_mem, d), bs,
        state_gdn_conv[0], state_gdn[0], state_rwkv[0], state_rwkv_shift[0],
        (pool_k, pool_v, page_table), p, w)

    y_p, conv_p, gdn_p, rwkv_p, shift_p, sbk_p, sbv_p = out_p
    y_s, conv_s, gdn_s, rwkv_s, shift_s, sbk_s, sbv_s = out_s
    mem_shape = (2, bp, n_mem, MEM_HEADS, d // MEM_HEADS)
    return (y_p, y_s, gdn_p, gdn_s, conv_p, conv_s, rwkv_p, rwkv_s, shift_p, shift_s,
            sbk_p, sbk_s, sbv_p, sbv_s, mem_k_p.reshape(mem_shape), mem_v_p.reshape(mem_shape))
```

```python
import functools
import math

import jax
import jax.numpy as jnp
from jax import lax
from jax.experimental import pallas as pl
from jax.experimental.pallas import tpu as pltpu

F32 = jnp.float32
BF16 = jnp.bfloat16

EPS = 1e-6
RWKV_GN_EPS = 64e-5
CONV_WIDTH = 4
GDN_HEADS = 4
GDN_HEAD_DIM = 128
GDN_WIDTH = GDN_HEADS * GDN_HEAD_DIM
GDN_CONV_CH = 3 * GDN_WIDTH
RWKV_HEAD_DIM = 64
RWKV_WIDTH = 512
RWKV_PAIRS = RWKV_WIDTH // 128
RWKV_IN = 1792
SB_HEADS = 16
SB_HEAD_DIM = 64
MEM_HEADS = 4
PAGE_SIZE = 128
LANES = 128
SUBLANES = 8
V7X_VMEM_LIMIT_BYTES = 56 * 1024 * 1024
PAGES_PER_STEP = 8


def _params(*sem):
    return pltpu.CompilerParams(dimension_semantics=sem, vmem_limit_bytes=V7X_VMEM_LIMIT_BYTES)


def _mm(a, b):
    return jnp.dot(a.astype(BF16), b.astype(BF16), preferred_element_type=F32)


def _mm_nt(a, b):
    return lax.dot_general(a.astype(BF16), b.astype(BF16), (((1,), (1,)), ((), ())),
                           preferred_element_type=F32)


def _mm_tn(a, b):
    return lax.dot_general(a.astype(BF16), b.astype(BF16), (((0,), (0,)), ((), ())),
                           preferred_element_type=F32)


def _hi_lo(x):
    hi = x.astype(BF16)
    lo = (x - hi.astype(F32)).astype(BF16)
    return hi, lo


def _mm3(a, b):
    ah, al = _hi_lo(a)
    bh, bl = _hi_lo(b)
    d = functools.partial(jnp.dot, preferred_element_type=F32)
    return d(ah, bh) + d(ah, bl) + d(al, bh)


INVERSE_PASSES = 1


def _mm_inv(a, b):
    return _mm3(a, b) if INVERSE_PASSES == 3 else _mm(a, b)


def _mm_exact_lhs(a_bf, x):
    xh, xl = _hi_lo(x)
    return (jnp.dot(a_bf, xh, preferred_element_type=F32)
            + jnp.dot(a_bf, xl, preferred_element_type=F32))


def _rms(x, g):
    return x * lax.rsqrt(jnp.mean(x * x, axis=-1, keepdims=True) + EPS) * g


def _softplus(x):
    return jnp.maximum(x, 0.0) + jnp.log1p(jnp.exp(-jnp.abs(x)))


def _iota2(shape, dim):
    return lax.broadcasted_iota(jnp.int32, shape, dim)


def _row_tile(m, want):
    t = min(m, want)
    while m % t:
        t //= 2
    return t


def _levels(c):
    return max(1, int(math.ceil(math.log2(c))))


def _norm_proj_kernel(x_ref, g_ref, *refs, n_w, outs):
    xn = _rms(x_ref[...], g_ref[...]).astype(BF16)
    prods = [jnp.dot(xn, w_ref[...], preferred_element_type=F32) for w_ref in refs[:n_w]]
    for (wi, dtype, scale, transposed), o_ref in zip(outs, refs[n_w:]):
        y = (prods[wi] if scale == 1.0 else prods[wi] * scale).astype(dtype)
        if transposed:
            o_ref[0] = y.T
        else:
            o_ref[...] = y


def _norm_proj(x2d, g, ws, outs=None, tm=512, seq_len=None):
    m, d = x2d.shape
    tm = _row_tile(seq_len if seq_len else m, tm)
    if outs is None:
        outs = [(i, F32, 1.0, False) for i in range(len(ws))]
    in_specs = [pl.BlockSpec((tm, d), lambda i: (i, 0)), pl.BlockSpec((1, d), lambda i: (0, 0))]
    in_specs += [pl.BlockSpec(w.shape, lambda i: (0, 0)) for w in ws]
    out_specs, out_shape = [], []
    for wi, dt, _, transposed in outs:
        n = ws[wi].shape[1]
        if transposed:
            nt = seq_len // tm
            out_specs.append(pl.BlockSpec((1, n, tm), lambda i, nt=nt: (i // nt, 0, i % nt)))
            out_shape.append(jax.ShapeDtypeStruct((m // seq_len, n, seq_len), dt))
        else:
            out_specs.append(pl.BlockSpec((tm, n), lambda i: (i, 0)))
            out_shape.append(jax.ShapeDtypeStruct((m, n), dt))
    return pl.pallas_call(
        functools.partial(_norm_proj_kernel, n_w=len(ws), outs=tuple(outs)),
        grid=(m // tm,), in_specs=in_specs, out_specs=out_specs, out_shape=out_shape,
        compiler_params=_params("parallel"), name="norm_proj",
    )(x2d, g.reshape(1, d), *ws)


def _proj_res_kernel(res_ref, *refs, n_in):
    acc = res_ref[...]
    for a_ref, w_ref in zip(refs[:n_in], refs[n_in:2 * n_in]):
        acc = acc + jnp.dot(a_ref[...].astype(BF16), w_ref[...], preferred_element_type=F32)
    refs[2 * n_in][...] = acc


def _proj_residual(res2d, acts, ws, tm=512):
    m, d = res2d.shape
    tm = _row_tile(m, tm)
    n_in = len(acts)
    in_specs = [pl.BlockSpec((tm, d), lambda i: (i, 0))]
    in_specs += [pl.BlockSpec((tm, a.shape[1]), lambda i: (i, 0)) for a in acts]
    in_specs += [pl.BlockSpec(w.shape, lambda i: (0, 0)) for w in ws]
    return pl.pallas_call(
        functools.partial(_proj_res_kernel, n_in=n_in),
        grid=(m // tm,), in_specs=in_specs,
        out_specs=pl.BlockSpec((tm, d), lambda i: (i, 0)),
        out_shape=jax.ShapeDtypeStruct((m, d), F32),
        compiler_params=_params("parallel"), name="proj_residual",
    )(res2d, *acts, *ws)


def _ffn_kernel(x_ref, g_ref, wg_ref, wu_ref, wo_ref, gf_ref, o_ref, xn_ref, acc_ref, *, final_norm):
    j = pl.program_id(1)

    @pl.when(j == 0)
    def _():
        xn_ref[...] = _rms(x_ref[...], g_ref[...]).astype(BF16)
        acc_ref[...] = jnp.zeros_like(acc_ref)

    xn = xn_ref[...]
    gt = jnp.dot(xn, wg_ref[...], preferred_element_type=F32)
    ut = jnp.dot(xn, wu_ref[...], preferred_element_type=F32)
    act = (gt * jax.nn.sigmoid(gt) * ut).astype(BF16)
    acc_ref[...] += jnp.dot(act, wo_ref[...], preferred_element_type=F32)

    @pl.when(j == pl.num_programs(1) - 1)
    def _():
        y = x_ref[...] + acc_ref[...]
        if final_norm:
            y = _rms(y, gf_ref[...])
        o_ref[...] = y


def _ffn(x2d, g, w_in, w_out, g_final, final_norm, tm=512, n_chunks=2):
    m, d = x2d.shape
    dff = w_out.shape[0]
    tf = dff // n_chunks
    tm = _row_tile(m, tm)
    return pl.pallas_call(
        functools.partial(_ffn_kernel, final_norm=final_norm),
        grid=(m // tm, n_chunks),
        in_specs=[
            pl.BlockSpec((tm, d), lambda i, j: (i, 0)),
            pl.BlockSpec((1, d), lambda i, j: (0, 0)),
            pl.BlockSpec((d, tf), lambda i, j: (0, j)),
            pl.BlockSpec((d, tf), lambda i, j: (0, n_chunks + j)),
            pl.BlockSpec((tf, d), lambda i, j: (j, 0)),
            pl.BlockSpec((1, d), lambda i, j: (0, 0)),
        ],
        out_specs=pl.BlockSpec((tm, d), lambda i, j: (i, 0)),
        out_shape=jax.ShapeDtypeStruct((m, d), F32),
        scratch_shapes=[pltpu.VMEM((tm, d), BF16), pltpu.VMEM((tm, d), F32)],
        compiler_params=_params("parallel", "arbitrary"), name="ffn",
    )(x2d, g.reshape(1, d), w_in, w_in, w_out, g_final.reshape(1, d))


def _mem_attn_kernel(x_ref, g_ref, wq_ref, wo_ref, mk_ref, mv_ref, o_ref, *, scale, heads):
    x = x_ref[0]
    xn = _rms(x, g_ref[...]).astype(BF16)
    q = (jnp.dot(xn, wq_ref[...], preferred_element_type=F32) * scale).astype(BF16)
    mk = mk_ref[0].astype(BF16)
    mv = mv_ref[0].astype(BF16)
    hd = x.shape[-1] // heads
    outs = []
    for h in range(heads):
        sl = slice(h * hd, (h + 1) * hd)
        s = _mm_nt(q[:, sl], mk[:, sl])
        e = jnp.exp(s - jnp.max(s, axis=-1, keepdims=True))
        outs.append(_mm(e, mv[:, sl]) / jnp.sum(e, axis=-1, keepdims=True))
    o = jnp.concatenate(outs, axis=-1).astype(BF16)
    o_ref[0] = x + jnp.dot(o, wo_ref[...], preferred_element_type=F32)


def _mem_attn(x, g, wq, wo, mk, mv, mem_base, tm=512):
    b, t, d = x.shape
    tm = _row_tile(t, tm)
    n_mem = mk.shape[1]
    return pl.pallas_call(
        functools.partial(_mem_attn_kernel, scale=(d // MEM_HEADS) ** -0.5, heads=MEM_HEADS),
        grid=(b, t // tm),
        in_specs=[
            pl.BlockSpec((1, tm, d), lambda i, j: (i, j, 0)),
            pl.BlockSpec((1, d), lambda i, j: (0, 0)),
            pl.BlockSpec((d, d), lambda i, j: (0, 0)),
            pl.BlockSpec((d, d), lambda i, j: (0, 0)),
            pl.BlockSpec((1, n_mem, d), lambda i, j: (mem_base + i, 0, 0)),
            pl.BlockSpec((1, n_mem, d), lambda i, j: (mem_base + i, 0, 0)),
        ],
        out_specs=pl.BlockSpec((1, tm, d), lambda i, j: (i, j, 0)),
        out_shape=jax.ShapeDtypeStruct((b, t, d), F32),
        compiler_params=_params("parallel", "parallel"), name="mem_attn",
    )(x, g.reshape(1, d), wq, wo, mk, mv)


def _gdn_kernel(gp_ref, bg_ref, cb_ref, s0_ref, cw_ref, hp_ref, gn_ref, o_ref, so_ref, cv_ref,
                ext_ref, s_ref, *, chunk, levels, nb):
    c = pl.program_id(1)
    C = chunk
    dk = GDN_HEAD_DIM
    tail = CONV_WIDTH - 1

    @pl.when(c == 0)
    def _():
        ext_ref[:, SUBLANES - tail:SUBLANES, :] = cb_ref[...]
        s_ref[...] = s0_ref[...]

    C2 = 2 * C
    r2 = _iota2((C2, C2), 0)
    c2 = _iota2((C2, C2), 1)
    same = (r2 >= C) == (c2 >= C)
    lower = same & (r2 >= c2)
    strict = same & (r2 > c2)
    tri = lower.astype(BF16)
    block_ones = same.astype(BF16)
    eye = (r2 == c2).astype(F32)
    first = _iota2((C2, dk), 0) < C
    lane = _iota2((C, LANES), 1)
    cw = cw_ref[...]

    groups = []
    for bi in range(nb):
        gp = gp_ref[bi]
        u_in = gp[:, :GDN_CONV_CH]
        ext_ref[bi, SUBLANES:SUBLANES + C, :] = u_in
        conv = u_in * cw[tail:tail + 1]
        for j in range(tail):
            lo = SUBLANES - tail + j
            conv = conv + ext_ref[bi, lo:lo + C, :] * cw[j:j + 1]
        new_tail = ext_ref[bi, SUBLANES + C - tail:SUBLANES + C, :]
        ext_ref[bi, SUBLANES - tail:SUBLANES, :] = new_tail
        cv_ref[bi] = new_tail
        qkv = conv * jax.nn.sigmoid(conv)

        bg = bg_ref[bi]
        beta_all = jax.nn.sigmoid(bg)
        g_all = -jnp.exp(hp_ref[0:1, :]) * _softplus(bg + hp_ref[1:2, :])

        def head(h, qkv=qkv, beta_all=beta_all, g_all=g_all):
            qh = qkv[:, h * dk:(h + 1) * dk]
            kh = qkv[:, GDN_WIDTH + h * dk:GDN_WIDTH + (h + 1) * dk]
            vh = qkv[:, 2 * GDN_WIDTH + h * dk:2 * GDN_WIDTH + (h + 1) * dk]
            qn = qh * lax.rsqrt(jnp.sum(qh * qh, axis=-1, keepdims=True) + EPS) * (dk ** -0.5)
            kn = kh * lax.rsqrt(jnp.sum(kh * kh, axis=-1, keepdims=True) + EPS)
            beta = jnp.sum(jnp.where(lane == h, beta_all, 0.0), axis=-1, keepdims=True)
            g = jnp.sum(jnp.where(lane == GDN_HEADS + h, g_all, 0.0), axis=-1, keepdims=True)
            return qn, kn, vh, beta, g

        for gi in range(GDN_HEADS // 2):
            h0, h1 = head(2 * gi), head(2 * gi + 1)
            qs, ks, vs, beta, g = (jnp.concatenate([a, b], axis=0) for a, b in zip(h0, h1))
            gcum = _mm_exact_lhs(tri, jnp.broadcast_to(g, (C2, dk)))
            g_cc = jnp.broadcast_to(g, (C2, C2))
            gcum_i = _mm_exact_lhs(tri, g_cc)
            gcum_j = _mm_exact_lhs(block_ones, jnp.where(same & (r2 <= c2), g_cc, 0.0))
            decay = jnp.where(lower, jnp.exp(jnp.minimum(gcum_i - gcum_j, 0.0)), 0.0)
            gam = jnp.exp(gcum)
            glast = jnp.where(first, gcum[C - 1:C, :], gcum[C2 - 1:C2, :])
            x = jnp.where(strict, -(beta * _mm_nt(ks, ks) * decay), 0.0)
            groups.append(dict(bi=bi, gi=gi, gp=gp, qs=qs, ks=ks, vs=vs, beta=beta, gam=gam, glast=glast,
                               gcum=gcum, x=x, qkd=_mm_nt(qs, ks) * decay))

    invs = [eye + gr["x"] for gr in groups]
    pws = [gr["x"] for gr in groups]
    for _ in range(levels - 1):
        pws = [_mm_inv(pw, pw) for pw in pws]
        invs = [inv + _mm_inv(inv, pw) for inv, pw in zip(invs, pws)]

    heads = [(n, j, slice(j * C, (j + 1) * C)) for n in range(len(groups)) for j in range(2)]
    w_mats = [_mm(inv, gr["beta"] * gr["gam"] * gr["ks"]) for gr, inv in zip(groups, invs)]
    u0s = [_mm(inv, gr["beta"] * gr["vs"]) for gr, inv in zip(groups, invs)]
    kdecs = [gr["ks"] * jnp.exp(gr["glast"] - gr["gcum"]) for gr in groups]
    states = [s_ref[groups[n]["bi"], 2 * groups[n]["gi"] + j] for n, j, _ in heads]
    ws_s = [_mm(w_mats[n][rows], s) for (n, _, rows), s in zip(heads, states)]
    qs_s = [_mm(groups[n]["qs"][rows], s) for (n, _, rows), s in zip(heads, states)]
    us = [u0s[n][rows] - ws for (n, _, rows), ws in zip(heads, ws_s)]
    u_g = [jnp.concatenate(us[2 * n:2 * n + 2], axis=0) for n in range(len(groups))]
    o_g = [gr["gam"] * jnp.concatenate(qs_s[2 * n:2 * n + 2], axis=0) + _mm(gr["qkd"], u_g[n])
           for n, gr in enumerate(groups)]
    upds = [_mm_tn(kdecs[n][rows], u) for (n, _, rows), u in zip(heads, us)]
    for (n, j, rows), s, upd in zip(heads, states, upds):
        gr = groups[n]
        bi, h = gr["bi"], 2 * gr["gi"] + j
        s_ref[bi, h] = jnp.exp(gr["glast"][j * C:j * C + 1, :]) * s + upd
        zh = gr["gp"][:, GDN_CONV_CH + h * dk:GDN_CONV_CH + (h + 1) * dk]
        o_ref[bi, :, h * dk:(h + 1) * dk] = _rms(o_g[n][rows], gn_ref[...]) * (zh * jax.nn.sigmoid(zh))

    @pl.when(c == pl.num_programs(1) - 1)
    def _():
        so_ref[...] = s_ref[...]


def _seqs_per_step(b):
    for nb in (4, 2):
        if b % nb == 0:
            return nb
    return 1


def _gdn(gp, bg, conv_buf, s0, conv_w, head_params, gnorm):
    b, t, _ = gp.shape
    chunk = min(64, t)
    nb = _seqs_per_step(b)
    assert t % chunk == 0 and chunk % SUBLANES == 0 and chunk >= CONV_WIDTH - 1
    state = (nb, GDN_HEADS, GDN_HEAD_DIM, GDN_HEAD_DIM)
    return pl.pallas_call(
        functools.partial(_gdn_kernel, chunk=chunk, levels=_levels(chunk), nb=nb),
        grid=(b // nb, t // chunk),
        in_specs=[
            pl.BlockSpec((nb, chunk, gp.shape[2]), lambda i, j: (i, j, 0)),
            pl.BlockSpec((nb, chunk, LANES), lambda i, j: (i, j, 0)),
            pl.BlockSpec((nb, CONV_WIDTH - 1, GDN_CONV_CH), lambda i, j: (i, 0, 0)),
            pl.BlockSpec(state, lambda i, j: (i, 0, 0, 0)),
            pl.BlockSpec((CONV_WIDTH, GDN_CONV_CH), lambda i, j: (0, 0)),
            pl.BlockSpec((SUBLANES, LANES), lambda i, j: (0, 0)),
            pl.BlockSpec((1, GDN_HEAD_DIM), lambda i, j: (0, 0)),
        ],
        out_specs=[
            pl.BlockSpec((nb, chunk, GDN_WIDTH), lambda i, j: (i, j, 0)),
            pl.BlockSpec(state, lambda i, j: (i, 0, 0, 0)),
            pl.BlockSpec((nb, CONV_WIDTH - 1, GDN_CONV_CH), lambda i, j: (i, 0, 0)),
        ],
        out_shape=[
            jax.ShapeDtypeStruct((b, t, GDN_WIDTH), F32),
            jax.ShapeDtypeStruct((b, GDN_HEADS, GDN_HEAD_DIM, GDN_HEAD_DIM), F32),
            jax.ShapeDtypeStruct((b, CONV_WIDTH - 1, GDN_CONV_CH), F32),
        ],
        scratch_shapes=[
            pltpu.VMEM((nb, chunk + SUBLANES, GDN_CONV_CH), F32),
            pltpu.VMEM(state, F32),
        ],
        compiler_params=_params("parallel", "arbitrary"), name="gdn",
    )(gp, bg, conv_buf, s0, conv_w, head_params, gnorm.reshape(1, GDN_HEAD_DIM))


def _rwkv_kernel(rp_ref, sh_ref, s0_ref, mu_ref, vec_ref, w2a_ref, g2_ref, o_ref, so_ref, sho_ref,
                 last_ref, s_ref, *, chunk, levels, nb):
    c = pl.program_id(1)
    C = chunk
    W = RWKV_WIDTH
    hd = RWKV_HEAD_DIM

    @pl.when(c == 0)
    def _():
        last_ref[...] = sh_ref[...]
        s_ref[...] = s0_ref[...]

    vec = vec_ref[...]
    lane = _iota2((C, LANES), 1)
    m0 = lane < hd
    rowi = _iota2((C, RWKV_IN), 0)

    r128 = _iota2((LANES, LANES), 0)
    c128 = _iota2((LANES, LANES), 1)
    bd_mask = (r128 >= hd) == (c128 >= hd)
    bd = bd_mask.astype(BF16)
    rc = _iota2((C, C), 0)
    cc = _iota2((C, C), 1)
    tri = (rc >= cc).astype(BF16)
    r2 = _iota2((2 * C, 2 * C), 0)
    c2 = _iota2((2 * C, 2 * C), 1)
    same = (r2 >= C) == (c2 >= C)
    strict = same & (r2 > c2)
    incl = same & (r2 >= c2)
    eye2 = (r2 == c2).astype(F32)

    def stack(x):
        return jnp.concatenate([x, x], axis=0)

    def stack_masked(x):
        return jnp.concatenate([jnp.where(m0, x, 0.0), jnp.where(m0, 0.0, x)], axis=0)

    def sel(z):
        return jnp.where(m0, z[:C], z[C:])

    pairs = []
    for bi in range(nb):
        rp = rp_ref[bi]
        prev = jnp.where(rowi == 0, last_ref[bi], pltpu.roll(rp, 1, 0))
        xr = rp + (prev - rp) * mu_ref[...]
        last_row = rp[C - 1:C, :]
        last_ref[bi] = last_row
        sho_ref[bi] = last_row

        r_all = xr[:, :W]
        kr = xr[:, W:2 * W]
        v_all = xr[:, 2 * W:3 * W]
        pwa = xr[:, 3 * W:3 * W + LANES]
        pg = xr[:, 3 * W + LANES:]
        wa = _mm(jnp.where(m0, jnp.tanh(pwa), pwa), w2a_ref[...])
        w_raw = vec[0:1] + wa[:, :W]
        a_all = jax.nn.sigmoid(vec[1:2] + wa[:, W:])
        lw_all = -jnp.exp(-_softplus(-w_raw) - 0.5)
        gate = _mm(jax.nn.sigmoid(pg), g2_ref[...])
        kkp = kr * vec[2:3]
        k2_all = kr * (1.0 + (a_all - 1.0) * vec[3:4])

        for p in range(RWKV_PAIRS):
            sl = slice(p * LANES, (p + 1) * LANES)
            r = r_all[:, sl]
            k2 = k2_all[:, sl]
            v = v_all[:, sl]
            lw = lw_all[:, sl]
            kk0 = kkp[:, sl]
            kk = kk0 * lax.rsqrt(_mm(kk0 * kk0, bd) + EPS)

            gcum = _mm_exact_lhs(tri, lw)
            e_in = jnp.exp(gcum)
            e_inv = jnp.exp(-gcum)
            abar = -kk * jnp.exp(gcum - lw)
            bbar = kk * a_all[:, sl] * e_inv
            kbar = k2 * e_inv
            rbar = r * e_in

            a_s = stack_masked(abar)
            r_s = stack_masked(rbar)
            b_c = stack(bbar)
            k_c = stack(kbar)
            pairs.append(dict(
                bi=bi, p=p, sl=sl, r=r, k2=k2, v=v, abar=abar, bbar=bbar, kbar=kbar, rbar=rbar,
                e_last=e_in[C - 1:C, :], gate=gate[:, sl],
                l_ab=jnp.where(strict, _mm_nt(a_s, b_c), 0.0), l_ak=jnp.where(strict, _mm_nt(a_s, k_c), 0.0),
                m_rb=jnp.where(incl, _mm_nt(r_s, b_c), 0.0), m_rk=jnp.where(incl, _mm_nt(r_s, k_c), 0.0)))

    invs = [eye2 + pr["l_ab"] for pr in pairs]
    pws = [pr["l_ab"] for pr in pairs]
    for _ in range(levels - 1):
        pws = [_mm_inv(pw, pw) for pw in pws]
        invs = [inv + _mm_inv(inv, pw) for inv, pw in zip(invs, pws)]

    inv_n = 1.0 / hd
    v_cs = [stack(pr["v"]) for pr in pairs]
    lakv = [_mm(pr["l_ak"], v_c) for pr, v_c in zip(pairs, v_cs)]
    mrkv = [_mm(pr["m_rk"], v_c) for pr, v_c in zip(pairs, v_cs)]
    bonus = [_mm(pr["r"] * pr["k2"] * vec[4:5, pr["sl"]], bd) * pr["v"] for pr in pairs]
    states = [s_ref[pr["bi"], pr["p"]] for pr in pairs]
    ars = [_mm_nt(jnp.concatenate([pr["abar"], pr["rbar"]], axis=0), s) for pr, s in zip(pairs, states)]
    rhss = [ar[:C] + sel(x) for ar, x in zip(ars, lakv)]
    us = [sel(_mm(inv, stack(rhs))) for inv, rhs in zip(invs, rhss)]
    ys = [ar[C:] + sel(_mm(pr["m_rb"], stack(u)) + x) for ar, pr, u, x in zip(ars, pairs, us, mrkv)]
    upds = [_mm_tn(jnp.concatenate([u, pr["v"]], axis=0), jnp.concatenate([pr["bbar"], pr["kbar"]], axis=0))
            for u, pr in zip(us, pairs)]
    for pr, s, upd in zip(pairs, states, upds):
        s_ref[pr["bi"], pr["p"]] = (s + jnp.where(bd_mask, upd, 0.0)) * pr["e_last"]
    ycs = [y - _mm(y, bd) * inv_n for y in ys]
    vrs = [_mm(yc * yc, bd) * inv_n for yc in ycs]
    for pr, yc, var, bon in zip(pairs, ycs, vrs, bonus):
        sl = pr["sl"]
        yn = yc * lax.rsqrt(var + RWKV_GN_EPS) * vec[5:6, sl] + vec[6:7, sl]
        o_ref[pr["bi"], :, sl] = (yn + bon) * pr["gate"]

    @pl.when(c == pl.num_programs(1) - 1)
    def _():
        so_ref[...] = s_ref[...]


def _rwkv(rp, shift, s0_pairs, mu, vecs, w2a, g2):
    b, t, _ = rp.shape
    chunk = min(64, t)
    nb = _seqs_per_step(b)
    assert t % chunk == 0 and chunk % SUBLANES == 0
    state = (nb, RWKV_PAIRS, LANES, LANES)
    return pl.pallas_call(
        functools.partial(_rwkv_kernel, chunk=chunk, levels=_levels(chunk), nb=nb),
        grid=(b // nb, t // chunk),
        in_specs=[
            pl.BlockSpec((nb, chunk, RWKV_IN), lambda i, j: (i, j, 0)),
            pl.BlockSpec((nb, 1, RWKV_IN), lambda i, j: (i, 0, 0)),
            pl.BlockSpec(state, lambda i, j: (i, 0, 0, 0)),
            pl.BlockSpec((1, RWKV_IN), lambda i, j: (0, 0)),
            pl.BlockSpec((SUBLANES, RWKV_WIDTH), lambda i, j: (0, 0)),
            pl.BlockSpec(w2a.shape, lambda i, j: (0, 0)),
            pl.BlockSpec(g2.shape, lambda i, j: (0, 0)),
        ],
        out_specs=[
            pl.BlockSpec((nb, chunk, RWKV_WIDTH), lambda i, j: (i, j, 0)),
            pl.BlockSpec(state, lambda i, j: (i, 0, 0, 0)),
            pl.BlockSpec((nb, 1, RWKV_IN), lambda i, j: (i, 0, 0)),
        ],
        out_shape=[
            jax.ShapeDtypeStruct((b, t, RWKV_WIDTH), F32),
            jax.ShapeDtypeStruct((b, RWKV_PAIRS, LANES, LANES), F32),
            jax.ShapeDtypeStruct((b, 1, RWKV_IN), F32),
        ],
        scratch_shapes=[
            pltpu.VMEM((nb, 1, RWKV_IN), F32),
            pltpu.VMEM(state, F32),
        ],
        compiler_params=_params("parallel", "arbitrary"), name="rwkv7",
    )(rp, shift.reshape(b, 1, RWKV_IN), s0_pairs, mu.reshape(1, RWKV_IN), vecs, w2a, g2)


def _sb_tile(z, r_later, u_bf, mask):
    (att,), (r_new,) = _sb_tiles([z], [r_later], u_bf, mask, chained=False)
    return att, r_new


def _sb_tiles(zs, r_in, u_bf, mask, chained):
    sps = [jnp.maximum(z, 0.0) + jnp.log(1.0 + jnp.exp(-jnp.abs(z))) for z in zs]
    if mask is not None:
        sps = [jnp.where(mask, sp, 0.0) for sp in sps]
    cums = [jnp.dot(sp.astype(BF16), u_bf, preferred_element_type=F32) for sp in sps]
    sums = [jnp.sum(sp, axis=-1, keepdims=True) for sp in sps]
    if chained:
        rs = [r_in[0]]
        for s in sums:
            rs.append(rs[-1] + s)
        r_tiles, r_out = rs[:-1], rs[-1:]
    else:
        r_tiles = r_in
        r_out = [r + s for r, s in zip(r_in, sums)]
    atts = [jnp.exp(z - cum - r) for z, cum, r in zip(zs, cums, r_tiles)]
    if mask is not None:
        atts = [jnp.where(mask, att, 0.0) for att in atts]
    return atts, r_out


def _sb_prompt_kernel(bias_ref, q_ref, k_ref, v_ref, o_ref, acc_ref, *, blk, kt, pg):
    g = pl.program_id(1)
    qi = pl.program_id(2)
    hd = SB_HEAD_DIM
    m0 = _iota2((blk, LANES), 1) < hd
    u_bf = (_iota2((kt, kt), 0) >= _iota2((kt, kt), 1)).astype(BF16)
    ktop = (qi * blk) // kt
    causal = (ktop * kt + _iota2((blk, kt), 1)) < (qi * blk + _iota2((blk, kt), 0))
    qs, bias = [], []
    for pp in range(pg):
        qp = q_ref[0, :, pp * LANES:(pp + 1) * LANES]
        zero = jnp.zeros_like(qp)
        qs.append((jnp.where(m0, qp, zero), jnp.where(m0, zero, qp)))
        head = 2 * (g * pg + pp)
        bias.append((bias_ref[head], bias_ref[head + 1]))
    acc_ref[...] = jnp.zeros_like(acc_ref)

    def tile(kj, r_later, mask):
        start = pl.multiple_of(kj * kt, kt)
        kbs = [k_ref[0, pl.ds(start, kt), pp * LANES:(pp + 1) * LANES] for pp in range(pg)]
        vbs = [v_ref[0, pl.ds(start, kt), pp * LANES:(pp + 1) * LANES] for pp in range(pg)]
        zs = [lax.dot_general(qs[pp][h], kbs[pp], (((1,), (1,)), ((), ())),
                              preferred_element_type=F32) + bias[pp][h]
              for pp in range(pg) for h in range(2)]
        atts, r_new = _sb_tiles(zs, list(r_later), u_bf, mask, chained=False)
        pvs = [jnp.dot(att.astype(BF16), vbs[i // 2], preferred_element_type=F32)
               for i, att in enumerate(atts)]
        for pp in range(pg):
            acc_ref[pp] += jnp.where(m0, pvs[2 * pp], pvs[2 * pp + 1])
        return tuple(r_new)

    r_later = tuple(jnp.zeros((blk, 1), F32) for _ in range(2 * pg))
    r_later = tile(ktop, r_later, causal)
    lax.fori_loop(0, ktop, lambda i, r: tile(ktop - 1 - i, r, None), r_later)
    for pp in range(pg):
        o_ref[0, :, pp * LANES:(pp + 1) * LANES] = acc_ref[pp]


def _sb_prompt(q, k, v, bias, pg=4):
    b, t, d = q.shape
    blk = 128 if t % 128 == 0 else t
    kt = 256 if t % 256 == 0 else blk
    assert blk % (2 * SUBLANES) == 0 and kt % blk == 0 and (d // LANES) % pg == 0
    w = pg * LANES
    return pl.pallas_call(
        functools.partial(_sb_prompt_kernel, blk=blk, kt=kt, pg=pg),
        grid=(b, d // w, t // blk),
        in_specs=[
            pl.BlockSpec(memory_space=pltpu.SMEM),
            pl.BlockSpec((1, blk, w), lambda i, p, j: (i, j, p)),
            pl.BlockSpec((1, t, w), lambda i, p, j: (i, 0, p)),
            pl.BlockSpec((1, t, w), lambda i, p, j: (i, 0, p)),
        ],
        out_specs=pl.BlockSpec((1, blk, w), lambda i, p, j: (i, j, p)),
        out_shape=jax.ShapeDtypeStruct((b, t, d), F32),
        scratch_shapes=[pltpu.VMEM((pg, blk, LANES), F32)],
        compiler_params=_params("parallel", "parallel", "arbitrary"), name="sb_prompt",
    )(bias, q, k, v)


def _sb_paged_kernel(pt_ref, bias_ref, q_ref, kn_ref, vn_ref, *refs, scale, pps, tq):
    k_refs = refs[:pps]
    v_refs = refs[pps:2 * pps]
    o_ref, qx_ref, acc_ref, r_ref = refs[2 * pps:]
    s = pl.program_id(1)
    hd = SB_HEAD_DIM
    rows = SB_HEADS * tq
    d = SB_HEADS * hd

    @pl.when(s == 0)
    def _():
        q = q_ref[0] * scale
        qt = jnp.concatenate([q] * SB_HEADS, axis=0)
        rh = _iota2((rows, d), 0) // tq
        lh = _iota2((rows, d), 1) // hd
        qx_ref[...] = jnp.where(rh == lh, qt, 0.0).astype(BF16)
        z = _mm_nt(qx_ref[...], kn_ref[0]) + bias_ref[:, :tq]
        rr = _iota2((rows, tq), 0)
        cc = _iota2((rows, tq), 1)
        mask = cc < (rr & (tq - 1))
        ur = _iota2((tq, tq), 0)
        uc = _iota2((tq, tq), 1)
        att, r_later = _sb_tile(z, jnp.zeros((rows, 1), F32), (ur >= uc).astype(BF16), mask)
        acc_ref[...] = _mm(att, vn_ref[0])
        r_ref[...] = jnp.broadcast_to(r_later, r_ref.shape)

    ur = _iota2((PAGE_SIZE, PAGE_SIZE), 0)
    uc = _iota2((PAGE_SIZE, PAGE_SIZE), 1)
    u_bf = (ur >= uc).astype(BF16)
    qx = qx_ref[...]
    zs = [_mm(qx, k_refs[j][0]) + bias_ref[...] for j in range(pps)]
    atts, (r_later,) = _sb_tiles(zs, [r_ref[...]], u_bf, None, chained=True)
    acc = acc_ref[...]
    for j in range(pps):
        acc = acc + _mm_nt(atts[j], v_refs[j][0])
    acc_ref[...] = acc
    r_ref[...] = r_later

    @pl.when(s == pl.num_programs(1) - 1)
    def _():
        lh = _iota2((tq, d), 1) // hd
        out = jnp.zeros((tq, d), F32)
        for h in range(SB_HEADS):
            out = out + jnp.where(lh == h, acc[h * tq:(h + 1) * tq, :], 0.0)
        o_ref[0] = out


def _sb_paged(q, k_new, v_new, pool_k, pool_v, page_table, pool_base, bias):
    b, tq, d = q.shape
    n_pages = page_table.shape[1]
    pps = PAGES_PER_STEP
    while n_pages % pps:
        pps //= 2
    assert tq & (tq - 1) == 0 and (SB_HEADS * tq) % SUBLANES == 0
    rows = SB_HEADS * tq
    bias_rows = jnp.broadcast_to(jnp.repeat(bias, tq)[:, None], (rows, PAGE_SIZE))

    def page_map(j):
        return lambda i, s, pt: (pool_base + pt[i, n_pages - 1 - (s * pps + j)], 0, 0)

    page_specs = [pl.BlockSpec((1, d, PAGE_SIZE), page_map(j)) for j in range(pps)]
    row_spec = pl.BlockSpec((1, tq, d), lambda i, s, pt: (i, 0, 0))
    grid_spec = pltpu.PrefetchScalarGridSpec(
        num_scalar_prefetch=1,
        grid=(b, n_pages // pps),
        in_specs=[pl.BlockSpec((rows, PAGE_SIZE), lambda i, s, pt: (0, 0)), row_spec, row_spec, row_spec]
        + page_specs + page_specs,
        out_specs=row_spec,
        scratch_shapes=[pltpu.VMEM((rows, d), BF16), pltpu.VMEM((rows, d), F32),
                        pltpu.VMEM((rows, PAGE_SIZE), F32)],
    )
    return pl.pallas_call(
        functools.partial(_sb_paged_kernel, scale=SB_HEAD_DIM ** -0.5, pps=pps, tq=tq),
        grid_spec=grid_spec,
        out_shape=jax.ShapeDtypeStruct((b, tq, d), F32),
        compiler_params=_params("parallel", "arbitrary"), name="sb_paged",
    )(page_table, bias_rows, q, k_new, v_new, *([pool_k] * pps), *([pool_v] * pps))


def _rwkv_state_to_pairs(s):
    b = s.shape[0]
    s = s.reshape(b, RWKV_PAIRS, 2, RWKV_HEAD_DIM, RWKV_HEAD_DIM)
    z = jnp.zeros_like(s[:, :, 0])
    top = jnp.concatenate([s[:, :, 0], z], axis=-1)
    bot = jnp.concatenate([z, s[:, :, 1]], axis=-1)
    return jnp.concatenate([top, bot], axis=-2)


def _rwkv_state_from_pairs(sp):
    b = sp.shape[0]
    hd = RWKV_HEAD_DIM
    return jnp.stack([sp[:, :, :hd, :hd], sp[:, :, hd:, hd:]], axis=2).reshape(b, 2 * RWKV_PAIRS, hd, hd)


def _prep_weights(p):
    w = {}
    ev = p["ev_w_in"][0]
    gdn_in = 4 * GDN_WIDTH
    w["ev_g"] = ev[:, :gdn_in].astype(BF16)
    w["ev_r"] = ev[:, gdn_in + 2 * GDN_HEADS:].astype(BF16)
    w["ev_bg"] = jnp.pad(ev[:, gdn_in:gdn_in + 2 * GDN_HEADS],
                         ((0, 0), (0, LANES - 2 * GDN_HEADS))).astype(BF16)
    w_out = p["ev_w_out"][0].astype(BF16)
    w["ev_out_a"], w["ev_out_b"] = w_out[:GDN_WIDTH], w_out[GDN_WIDTH:]
    hp = jnp.zeros((SUBLANES, LANES), F32)
    hp = hp.at[0, GDN_HEADS:2 * GDN_HEADS].set(p["gdn_a_log"][0])
    hp = hp.at[1, GDN_HEADS:2 * GDN_HEADS].set(p["gdn_dt_bias"][0])
    w["gdn_hp"] = hp
    names = ("rwkv_w0", "rwkv_a0", "rwkv_k_k", "rwkv_k_a", "rwkv_r_k", "rwkv_gn_g", "rwkv_gn_b")
    rows = [p[n][0] for n in names] + [jnp.zeros((RWKV_WIDTH,), F32)]
    w["rwkv_vecs"] = jnp.stack(rows)
    w2, a2 = p["rwkv_w2"][0], p["rwkv_a2"][0]
    zero = jnp.zeros_like(w2)
    w["rwkv_w2a"] = jnp.concatenate(
        [jnp.concatenate([w2, zero], axis=1), jnp.concatenate([zero, a2], axis=1)], axis=0).astype(BF16)
    w["rwkv_g2"] = p["rwkv_g2"][0].astype(BF16)
    sb = p["sb_w_in"][0].astype(BF16)
    d = sb.shape[0]
    w["sb_q"], w["sb_k"], w["sb_v"] = sb[:, :d], sb[:, d:2 * d], sb[:, 2 * d:]
    w["sb_out"] = p["sb_w_out"][0].astype(BF16)
    for n in ("mem_w_q", "mem_w_k", "mem_w_v", "mem_w_o", "ffn_w_in", "ffn_w_out"):
        w[n] = p[n].astype(BF16)
    return w


def _trunk(x, mem_k, mem_v, mem_stride, conv_buf, s_gdn, s_rwkv, shift, sb_past, p, w):
    b, t, d = x.shape
    m = b * t
    gp, rp, bg = _norm_proj(x.reshape(m, d), p["norm_mix"][0], [w["ev_g"], w["ev_r"], w["ev_bg"]])
    o_a, gdn_new, conv_new = _gdn(gp.reshape(b, t, -1), bg.reshape(b, t, LANES), conv_buf, s_gdn,
                                  p["gdn_conv_w"][0], w["gdn_hp"], p["gdn_norm"][0])
    o_b, rwkv_new, shift_new = _rwkv(rp.reshape(b, t, RWKV_IN), shift, _rwkv_state_to_pairs(s_rwkv),
                                     p["rwkv_mu"][0], w["rwkv_vecs"], w["rwkv_w2a"], w["rwkv_g2"])
    x2 = _proj_residual(x.reshape(m, d), [o_a.reshape(m, -1), o_b.reshape(m, -1)],
                        [w["ev_out_a"], w["ev_out_b"]])
    x3 = _mem_attn(x2.reshape(b, t, d), p["norm_mem"][0], w["mem_w_q"][0], w["mem_w_o"][0],
                   mem_k, mem_v, 0)
    x4 = _ffn(x3.reshape(m, d), p["norm_ffn"][0], w["ffn_w_in"][0], w["ffn_w_out"][0],
              p["norm_final"], False)
    sb_w = [w["sb_q"], w["sb_k"], w["sb_v"]]
    heads = (b, t, SB_HEADS, SB_HEAD_DIM)
    if sb_past is None:
        outs = [(0, BF16, SB_HEAD_DIM ** -0.5, False), (1, BF16, 1.0, False), (2, BF16, 1.0, False),
                (1, F32, 1.0, True), (2, F32, 1.0, True)]
        q_bf, k_bf, v_bf, k_t, v_t = _norm_proj(x4, p["norm_mix"][1], sb_w, outs, seq_len=t)
        y = _sb_prompt(q_bf.reshape(b, t, d), k_bf.reshape(b, t, d), v_bf.reshape(b, t, d), p["sb_bias"][0])
        k_out, v_out = (jnp.transpose(a.reshape(b, SB_HEADS, SB_HEAD_DIM, t), (0, 3, 1, 2)) for a in (k_t, v_t))
    else:
        q, k, v = (a.reshape(b, t, d) for a in _norm_proj(x4, p["norm_mix"][1], sb_w))
        pool_k, pool_v, page_table = sb_past
        y = _sb_paged(q, k, v, pool_k, pool_v, page_table, 0, p["sb_bias"][0])
        k_out, v_out = k.reshape(heads), v.reshape(heads)
    x5 = _proj_residual(x4, [y.reshape(m, d)], [w["sb_out"]])
    x6 = _mem_attn(x5.reshape(b, t, d), p["norm_mem"][1], w["mem_w_q"][1], w["mem_w_o"][1],
                   mem_k, mem_v, mem_stride)
    y_out = _ffn(x6.reshape(m, d), p["norm_ffn"][1], w["ffn_w_in"][1], w["ffn_w_out"][1],
                 p["norm_final"], True).reshape(b, t, d)
    return (y_out, conv_new[None], gdn_new[None], _rwkv_state_from_pairs(rwkv_new)[None],
            shift_new.reshape(b, RWKV_IN)[None], k_out[None], v_out[None])


def kernel(x_prompt, x_sample, mem_prompt, state_gdn, state_gdn_conv, state_rwkv, state_rwkv_shift,
           cache_sb_k, cache_sb_v, cache_mem_k, cache_mem_v, page_table,
           norm_mix, norm_mem, norm_memtok, norm_ffn, norm_final, ev_w_in, ev_w_out,
           gdn_conv_w, gdn_a_log, gdn_dt_bias, gdn_norm, rwkv_mu, rwkv_w0, rwkv_w2, rwkv_a0, rwkv_a2, rwkv_g2,
           rwkv_k_k, rwkv_k_a, rwkv_r_k, rwkv_gn_g, rwkv_gn_b, sb_w_in, sb_w_out, sb_bias,
           mem_w_q, mem_w_k, mem_w_v, mem_w_o, ffn_w_in, ffn_w_out):
    p = dict(norm_mix=norm_mix, norm_mem=norm_mem, norm_memtok=norm_memtok, norm_ffn=norm_ffn,
             norm_final=norm_final, ev_w_in=ev_w_in, ev_w_out=ev_w_out, gdn_conv_w=gdn_conv_w,
             gdn_a_log=gdn_a_log, gdn_dt_bias=gdn_dt_bias, gdn_norm=gdn_norm, rwkv_mu=rwkv_mu,
             rwkv_w0=rwkv_w0, rwkv_w2=rwkv_w2, rwkv_a0=rwkv_a0, rwkv_a2=rwkv_a2, rwkv_g2=rwkv_g2,
             rwkv_k_k=rwkv_k_k, rwkv_k_a=rwkv_k_a, rwkv_r_k=rwkv_r_k, rwkv_gn_g=rwkv_gn_g,
             rwkv_gn_b=rwkv_gn_b, sb_w_in=sb_w_in, sb_w_out=sb_w_out, sb_bias=sb_bias,
             mem_w_q=mem_w_q, mem_w_k=mem_w_k, mem_w_v=mem_w_v, mem_w_o=mem_w_o,
             ffn_w_in=ffn_w_in, ffn_w_out=ffn_w_out)
    assert ev_w_in.shape[0] == 1 and sb_w_in.shape[0] == 1 and norm_mix.shape[0] == 2
    w = _prep_weights(p)
    bp, tp, d = x_prompt.shape
    bs = x_sample.shape[0]
    n_mem = mem_prompt.shape[1]

    mem2d = mem_prompt.reshape(bp * n_mem, d)
    mk_l, mv_l = [], []
    for layer in range(2):
        mk, mv = _norm_proj(mem2d, norm_memtok[layer], [w["mem_w_k"][layer], w["mem_w_v"][layer]])
        mk_l.append(mk.reshape(bp, n_mem, d))
        mv_l.append(mv.reshape(bp, n_mem, d))
    mem_k_p = jnp.stack(mk_l)
    mem_v_p = jnp.stack(mv_l)
    zeros = functools.partial(jnp.zeros, dtype=F32)
    out_p = _trunk(
        x_prompt, mem_k_p.reshape(2 * bp, n_mem, d), mem_v_p.reshape(2 * bp, n_mem, d), bp,
        zeros((bp, CONV_WIDTH - 1, GDN_CONV_CH)), zeros((bp, GDN_HEADS, GDN_HEAD_DIM, GDN_HEAD_DIM)),
        zeros((bp, 2 * RWKV_PAIRS, RWKV_HEAD_DIM, RWKV_HEAD_DIM)), zeros((bp, RWKV_IN)), None, p, w)

    n_pool = cache_sb_k.shape[1]
    pool_k = jnp.transpose(cache_sb_k, (0, 1, 3, 4, 2)).reshape(cache_sb_k.shape[0] * n_pool, d, PAGE_SIZE)
    pool_v = jnp.transpose(cache_sb_v, (0, 1, 3, 4, 2)).reshape(cache_sb_v.shape[0] * n_pool, d, PAGE_SIZE)
    out_s = _trunk(
        x_sample, cache_mem_k.reshape(2 * bs, n_mem, d), cache_mem_v.reshape(2 * bs, n_mem, d), bs,
        state_gdn_conv[0], state_gdn[0], state_rwkv[0], state_rwkv_shift[0],
        (pool_k, pool_v, page_table), p, w)

    y_p, conv_p, gdn_p, rwkv_p, shift_p, sbk_p, sbv_p = out_p
    y_s, conv_s, gdn_s, rwkv_s, shift_s, sbk_s, sbv_s = out_s
    mem_shape = (2, bp, n_mem, MEM_HEADS, d // MEM_HEADS)
    return (y_p, y_s, gdn_p, gdn_s, conv_p, conv_s, rwkv_p, rwkv_s, shift_p, shift_s,
            sbk_p, sbk_s, sbv_p, sbv_s, mem_k_p.reshape(mem_shape), mem_v_p.reshape(mem_shape))
```

```python
import functools
import math

import jax
import jax.numpy as jnp
from jax import lax
from jax.experimental import pallas as pl
from jax.experimental.pallas import tpu as pltpu

F32 = jnp.float32
BF16 = jnp.bfloat16

EPS = 1e-6
RWKV_GN_EPS = 64e-5
CONV_WIDTH = 4
GDN_HEADS = 4
GDN_HEAD_DIM = 128
GDN_WIDTH = GDN_HEADS * GDN_HEAD_DIM
GDN_CONV_CH = 3 * GDN_WIDTH
RWKV_HEAD_DIM = 64
RWKV_WIDTH = 512
RWKV_PAIRS = RWKV_WIDTH // 128
RWKV_IN = 1792
SB_HEADS = 16
SB_HEAD_DIM = 64
MEM_HEADS = 4
PAGE_SIZE = 128
LANES = 128
SUBLANES = 8
V7X_VMEM_LIMIT_BYTES = 56 * 1024 * 1024
PAGES_PER_STEP = 16


def _params(*sem):
    return pltpu.CompilerParams(dimension_semantics=sem, vmem_limit_bytes=V7X_VMEM_LIMIT_BYTES)


def _mm(a, b):
    return jnp.dot(a.astype(BF16), b.astype(BF16), preferred_element_type=F32)


def _mm_nt(a, b):
    return lax.dot_general(a.astype(BF16), b.astype(BF16), (((1,), (1,)), ((), ())),
                           preferred_element_type=F32)


def _mm_tn(a, b):
    return lax.dot_general(a.astype(BF16), b.astype(BF16), (((0,), (0,)), ((), ())),
                           preferred_element_type=F32)


def _hi_lo(x):
    hi = x.astype(BF16)
    lo = (x - hi.astype(F32)).astype(BF16)
    return hi, lo


def _mm3(a, b):
    ah, al = _hi_lo(a)
    bh, bl = _hi_lo(b)
    d = functools.partial(jnp.dot, preferred_element_type=F32)
    return d(ah, bh) + d(ah, bl) + d(al, bh)


INVERSE_PASSES = 1


def _mm_inv(a, b):
    return _mm3(a, b) if INVERSE_PASSES == 3 else _mm(a, b)


def _mm_exact_lhs(a_bf, x):
    xh, xl = _hi_lo(x)
    return (jnp.dot(a_bf, xh, preferred_element_type=F32)
            + jnp.dot(a_bf, xl, preferred_element_type=F32))


def _rms(x, g):
    return x * lax.rsqrt(jnp.mean(x * x, axis=-1, keepdims=True) + EPS) * g


def _softplus(x):
    return jnp.maximum(x, 0.0) + jnp.log1p(jnp.exp(-jnp.abs(x)))


def _iota2(shape, dim):
    return lax.broadcasted_iota(jnp.int32, shape, dim)


def _row_tile(m, want):
    t = min(m, want)
    while m % t:
        t //= 2
    return t


def _levels(c):
    return max(1, int(math.ceil(math.log2(c))))


def _norm_proj_kernel(x_ref, g_ref, *refs, n_w, outs):
    xn = _rms(x_ref[...], g_ref[...]).astype(BF16)
    prods = [jnp.dot(xn, w_ref[...], preferred_element_type=F32) for w_ref in refs[:n_w]]
    for (wi, dtype, scale, transposed), o_ref in zip(outs, refs[n_w:]):
        y = (prods[wi] if scale == 1.0 else prods[wi] * scale).astype(dtype)
        if transposed:
            o_ref[0] = y.T
        else:
            o_ref[...] = y


def _norm_proj(x2d, g, ws, outs=None, tm=512, seq_len=None):
    m, d = x2d.shape
    tm = _row_tile(seq_len if seq_len else m, tm)
    if outs is None:
        outs = [(i, F32, 1.0, False) for i in range(len(ws))]
    in_specs = [pl.BlockSpec((tm, d), lambda i: (i, 0)), pl.BlockSpec((1, d), lambda i: (0, 0))]
    in_specs += [pl.BlockSpec(w.shape, lambda i: (0, 0)) for w in ws]
    out_specs, out_shape = [], []
    for wi, dt, _, transposed in outs:
        n = ws[wi].shape[1]
        if transposed:
            nt = seq_len // tm
            out_specs.append(pl.BlockSpec((1, n, tm), lambda i, nt=nt: (i // nt, 0, i % nt)))
            out_shape.append(jax.ShapeDtypeStruct((m // seq_len, n, seq_len), dt))
        else:
            out_specs.append(pl.BlockSpec((tm, n), lambda i: (i, 0)))
            out_shape.append(jax.ShapeDtypeStruct((m, n), dt))
    return pl.pallas_call(
        functools.partial(_norm_proj_kernel, n_w=len(ws), outs=tuple(outs)),
        grid=(m // tm,), in_specs=in_specs, out_specs=out_specs, out_shape=out_shape,
        compiler_params=_params("parallel"), name="norm_proj",
    )(x2d, g.reshape(1, d), *ws)


def _ffn_kernel(x_ref, g_ref, wg_ref, wu_ref, wo_ref, gf_ref, o_ref, xn_ref, acc_ref, *, final_norm):
    j = pl.program_id(1)

    @pl.when(j == 0)
    def _():
        xn_ref[...] = _rms(x_ref[...], g_ref[...]).astype(BF16)
        acc_ref[...] = jnp.zeros_like(acc_ref)

    xn = xn_ref[...]
    gt = jnp.dot(xn, wg_ref[...], preferred_element_type=F32)
    ut = jnp.dot(xn, wu_ref[...], preferred_element_type=F32)
    act = (gt * jax.nn.sigmoid(gt) * ut).astype(BF16)
    acc_ref[...] += jnp.dot(act, wo_ref[...], preferred_element_type=F32)

    @pl.when(j == pl.num_programs(1) - 1)
    def _():
        y = x_ref[...] + acc_ref[...]
        if final_norm:
            y = _rms(y, gf_ref[...])
        o_ref[...] = y


def _ffn(x2d, g, w_in, w_out, g_final, final_norm, tm=512, n_chunks=2):
    m, d = x2d.shape
    dff = w_out.shape[0]
    tf = dff // n_chunks
    tm = _row_tile(m, tm)
    return pl.pallas_call(
        functools.partial(_ffn_kernel, final_norm=final_norm),
        grid=(m // tm, n_chunks),
        in_specs=[
            pl.BlockSpec((tm, d), lambda i, j: (i, 0)),
            pl.BlockSpec((1, d), lambda i, j: (0, 0)),
            pl.BlockSpec((d, tf), lambda i, j: (0, j)),
            pl.BlockSpec((d, tf), lambda i, j: (0, n_chunks + j)),
            pl.BlockSpec((tf, d), lambda i, j: (j, 0)),
            pl.BlockSpec((1, d), lambda i, j: (0, 0)),
        ],
        out_specs=pl.BlockSpec((tm, d), lambda i, j: (i, 0)),
        out_shape=jax.ShapeDtypeStruct((m, d), F32),
        scratch_shapes=[pltpu.VMEM((tm, d), BF16), pltpu.VMEM((tm, d), F32)],
        compiler_params=_params("parallel", "arbitrary"), name="ffn",
    )(x2d, g.reshape(1, d), w_in, w_in, w_out, g_final.reshape(1, d))


def _mix_mem_attn_kernel(x_ref, *refs, scale, heads, n_in):
    act_refs, w_refs = refs[:n_in], refs[n_in:2 * n_in]
    g_ref, wq_ref, wo_ref, mk_ref, mv_ref, o_ref = refs[2 * n_in:]
    ns, tm, d = x_ref.shape
    x = x_ref[...].reshape(ns * tm, d)
    for a_ref, w_ref in zip(act_refs, w_refs):
        a = a_ref[...].reshape(ns * tm, a_ref.shape[2])
        x = x + jnp.dot(a.astype(BF16), w_ref[...], preferred_element_type=F32)
    xn = _rms(x, g_ref[...]).astype(BF16)
    q = (jnp.dot(xn, wq_ref[...], preferred_element_type=F32) * scale).astype(BF16)
    hd = d // heads
    units = [(s, h, slice(h * hd, (h + 1) * hd)) for s in range(ns) for h in range(heads)]
    mks = [mk_ref[s].astype(BF16) for s in range(ns)]
    mvs = [mv_ref[s].astype(BF16) for s in range(ns)]
    scores = [_mm_nt(q[s * tm:(s + 1) * tm, sl], mks[s][:, sl]) for s, _, sl in units]
    es = [jnp.exp(sc - jnp.max(sc, axis=-1, keepdims=True)) for sc in scores]
    pvs = [_mm(e, mvs[s][:, sl]) / jnp.sum(e, axis=-1, keepdims=True) for e, (s, _, sl) in zip(es, units)]
    o = jnp.concatenate([jnp.concatenate(pvs[s * heads:(s + 1) * heads], axis=-1) for s in range(ns)], axis=0)
    y = x + jnp.dot(o.astype(BF16), wo_ref[...], preferred_element_type=F32)
    o_ref[...] = y.reshape(ns, tm, d)


def _mix_mem_attn(x, acts, ws, g, wq, wo, mk, mv, mem_base, tm=512):
    b, t, d = x.shape
    tm = _row_tile(t, tm)
    ns = 1
    if tm == t:
        while ns < 4 and b % (2 * ns) == 0 and mem_base % (2 * ns) == 0 and 2 * ns * tm <= 512:
            ns *= 2
    n_mem = mk.shape[1]
    seq = lambda width: pl.BlockSpec((ns, tm, width), lambda i, j: (i, j, 0))
    const = lambda shape: pl.BlockSpec(shape, lambda i, j: (0,) * len(shape))
    mem = pl.BlockSpec((ns, n_mem, d), lambda i, j: (mem_base // ns + i, 0, 0))
    return pl.pallas_call(
        functools.partial(_mix_mem_attn_kernel, scale=(d // MEM_HEADS) ** -0.5, heads=MEM_HEADS,
                          n_in=len(acts)),
        grid=(b // ns, t // tm),
        in_specs=[seq(d)] + [seq(a.shape[2]) for a in acts] + [const(w_.shape) for w_ in ws]
        + [const((1, d)), const((d, d)), const((d, d)), mem, mem],
        out_specs=seq(d),
        out_shape=jax.ShapeDtypeStruct((b, t, d), F32),
        compiler_params=_params("parallel", "parallel"), name="mix_mem_attn",
    )(x, *acts, *ws, g.reshape(1, d), wq, wo, mk, mv)


def _gdn_kernel(gp_ref, bg_ref, cb_ref, s0_ref, cw_ref, hp_ref, gn_ref, o_ref, so_ref, cv_ref,
                ext_ref, s_ref, *, chunk, levels, nb):
    c = pl.program_id(1)
    C = chunk
    dk = GDN_HEAD_DIM
    tail = CONV_WIDTH - 1

    @pl.when(c == 0)
    def _():
        ext_ref[:, SUBLANES - tail:SUBLANES, :] = cb_ref[...]
        s_ref[...] = s0_ref[...]

    C2 = 2 * C
    r2 = _iota2((C2, C2), 0)
    c2 = _iota2((C2, C2), 1)
    same = (r2 >= C) == (c2 >= C)
    lower = same & (r2 >= c2)
    strict = same & (r2 > c2)
    tri = lower.astype(BF16)
    block_ones = same.astype(BF16)
    eye = (r2 == c2).astype(F32)
    first = _iota2((C2, dk), 0) < C
    lane = _iota2((C, LANES), 1)
    cw = cw_ref[...]

    groups = []
    for bi in range(nb):
        gp = gp_ref[bi]
        u_in = gp[:, :GDN_CONV_CH]
        ext_ref[bi, SUBLANES:SUBLANES + C, :] = u_in
        conv = u_in * cw[tail:tail + 1]
        for j in range(tail):
            lo = SUBLANES - tail + j
            conv = conv + ext_ref[bi, lo:lo + C, :] * cw[j:j + 1]
        new_tail = ext_ref[bi, SUBLANES + C - tail:SUBLANES + C, :]
        ext_ref[bi, SUBLANES - tail:SUBLANES, :] = new_tail
        cv_ref[bi] = new_tail
        qkv = conv * jax.nn.sigmoid(conv)

        bg = bg_ref[bi]
        beta_all = jax.nn.sigmoid(bg)
        g_all = -jnp.exp(hp_ref[0:1, :]) * _softplus(bg + hp_ref[1:2, :])

        def head(h, qkv=qkv, beta_all=beta_all, g_all=g_all):
            qh = qkv[:, h * dk:(h + 1) * dk]
            kh = qkv[:, GDN_WIDTH + h * dk:GDN_WIDTH + (h + 1) * dk]
            vh = qkv[:, 2 * GDN_WIDTH + h * dk:2 * GDN_WIDTH + (h + 1) * dk]
            qn = qh * lax.rsqrt(jnp.sum(qh * qh, axis=-1, keepdims=True) + EPS) * (dk ** -0.5)
            kn = kh * lax.rsqrt(jnp.sum(kh * kh, axis=-1, keepdims=True) + EPS)
            beta = jnp.sum(jnp.where(lane == h, beta_all, 0.0), axis=-1, keepdims=True)
            g = jnp.sum(jnp.where(lane == GDN_HEADS + h, g_all, 0.0), axis=-1, keepdims=True)
            return qn, kn, vh, beta, g

        for gi in range(GDN_HEADS // 2):
            h0, h1 = head(2 * gi), head(2 * gi + 1)
            qs, ks, vs, beta, g = (jnp.concatenate([a, b], axis=0) for a, b in zip(h0, h1))
            gcum = _mm_exact_lhs(tri, jnp.broadcast_to(g, (C2, dk)))
            g_cc = jnp.broadcast_to(g, (C2, C2))
            gcum_i = _mm_exact_lhs(tri, g_cc)
            gcum_j = _mm_exact_lhs(block_ones, jnp.where(same & (r2 <= c2), g_cc, 0.0))
            decay = jnp.where(lower, jnp.exp(jnp.minimum(gcum_i - gcum_j, 0.0)), 0.0)
            gam = jnp.exp(gcum)
            glast = jnp.where(first, gcum[C - 1:C, :], gcum[C2 - 1:C2, :])
            x = jnp.where(strict, -(beta * _mm_nt(ks, ks) * decay), 0.0)
            groups.append(dict(bi=bi, gi=gi, gp=gp, qs=qs, ks=ks, vs=vs, beta=beta, gam=gam, glast=glast,
                               gcum=gcum, x=x, qkd=_mm_nt(qs, ks) * decay))

    invs = [eye + gr["x"] for gr in groups]
    pws = [gr["x"] for gr in groups]
    for _ in range(levels - 1):
        pws = [_mm_inv(pw, pw) for pw in pws]
        invs = [inv + _mm_inv(inv, pw) for inv, pw in zip(invs, pws)]

    heads = [(n, j, slice(j * C, (j + 1) * C)) for n in range(len(groups)) for j in range(2)]
    w_mats = [_mm(inv, gr["beta"] * gr["gam"] * gr["ks"]) for gr, inv in zip(groups, invs)]
    u0s = [_mm(inv, gr["beta"] * gr["vs"]) for gr, inv in zip(groups, invs)]
    kdecs = [gr["ks"] * jnp.exp(gr["glast"] - gr["gcum"]) for gr in groups]
    states = [s_ref[groups[n]["bi"], 2 * groups[n]["gi"] + j] for n, j, _ in heads]
    ws_s = [_mm(w_mats[n][rows], s) for (n, _, rows), s in zip(heads, states)]
    qs_s = [_mm(groups[n]["qs"][rows], s) for (n, _, rows), s in zip(heads, states)]
    us = [u0s[n][rows] - ws for (n, _, rows), ws in zip(heads, ws_s)]
    u_g = [jnp.concatenate(us[2 * n:2 * n + 2], axis=0) for n in range(len(groups))]
    o_g = [gr["gam"] * jnp.concatenate(qs_s[2 * n:2 * n + 2], axis=0) + _mm(gr["qkd"], u_g[n])
           for n, gr in enumerate(groups)]
    upds = [_mm_tn(kdecs[n][rows], u) for (n, _, rows), u in zip(heads, us)]
    for (n, j, rows), s, upd in zip(heads, states, upds):
        gr = groups[n]
        bi, h = gr["bi"], 2 * gr["gi"] + j
        s_ref[bi, h] = jnp.exp(gr["glast"][j * C:j * C + 1, :]) * s + upd
        zh = gr["gp"][:, GDN_CONV_CH + h * dk:GDN_CONV_CH + (h + 1) * dk]
        o_ref[bi, :, h * dk:(h + 1) * dk] = _rms(o_g[n][rows], gn_ref[...]) * (zh * jax.nn.sigmoid(zh))

    @pl.when(c == pl.num_programs(1) - 1)
    def _():
        so_ref[...] = s_ref[...]


def _seqs_per_step(b):
    for nb in (8, 4, 2):
        if b % nb == 0:
            return nb
    return 1


def _gdn(gp, bg, conv_buf, s0, conv_w, head_params, gnorm):
    b, t, _ = gp.shape
    chunk = min(64, t)
    nb = _seqs_per_step(b)
    assert t % chunk == 0 and chunk % SUBLANES == 0 and chunk >= CONV_WIDTH - 1
    state = (nb, GDN_HEADS, GDN_HEAD_DIM, GDN_HEAD_DIM)
    return pl.pallas_call(
        functools.partial(_gdn_kernel, chunk=chunk, levels=_levels(chunk), nb=nb),
        grid=(b // nb, t // chunk),
        in_specs=[
            pl.BlockSpec((nb, chunk, gp.shape[2]), lambda i, j: (i, j, 0)),
            pl.BlockSpec((nb, chunk, LANES), lambda i, j: (i, j, 0)),
            pl.BlockSpec((nb, CONV_WIDTH - 1, GDN_CONV_CH), lambda i, j: (i, 0, 0)),
            pl.BlockSpec(state, lambda i, j: (i, 0, 0, 0)),
            pl.BlockSpec((CONV_WIDTH, GDN_CONV_CH), lambda i, j: (0, 0)),
            pl.BlockSpec((SUBLANES, LANES), lambda i, j: (0, 0)),
            pl.BlockSpec((1, GDN_HEAD_DIM), lambda i, j: (0, 0)),
        ],
        out_specs=[
            pl.BlockSpec((nb, chunk, GDN_WIDTH), lambda i, j: (i, j, 0)),
            pl.BlockSpec(state, lambda i, j: (i, 0, 0, 0)),
            pl.BlockSpec((nb, CONV_WIDTH - 1, GDN_CONV_CH), lambda i, j: (i, 0, 0)),
        ],
        out_shape=[
            jax.ShapeDtypeStruct((b, t, GDN_WIDTH), F32),
            jax.ShapeDtypeStruct((b, GDN_HEADS, GDN_HEAD_DIM, GDN_HEAD_DIM), F32),
            jax.ShapeDtypeStruct((b, CONV_WIDTH - 1, GDN_CONV_CH), F32),
        ],
        scratch_shapes=[
            pltpu.VMEM((nb, chunk + SUBLANES, GDN_CONV_CH), F32),
            pltpu.VMEM(state, F32),
        ],
        compiler_params=_params("parallel", "arbitrary"), name="gdn",
    )(gp, bg, conv_buf, s0, conv_w, head_params, gnorm.reshape(1, GDN_HEAD_DIM))


def _rwkv_kernel(rp_ref, sh_ref, s0_ref, mu_ref, vec_ref, w2a_ref, g2_ref, o_ref, so_ref, sho_ref,
                 last_ref, s_ref, *, chunk, levels, nb):
    c = pl.program_id(1)
    C = chunk
    W = RWKV_WIDTH
    hd = RWKV_HEAD_DIM

    @pl.when(c == 0)
    def _():
        last_ref[...] = sh_ref[...]
        s_ref[...] = s0_ref[...]

    vec = vec_ref[...]
    lane = _iota2((C, LANES), 1)
    m0 = lane < hd
    rowi = _iota2((C, RWKV_IN), 0)

    r128 = _iota2((LANES, LANES), 0)
    c128 = _iota2((LANES, LANES), 1)
    bd_mask = (r128 >= hd) == (c128 >= hd)
    bd = bd_mask.astype(BF16)
    rc = _iota2((C, C), 0)
    cc = _iota2((C, C), 1)
    tri = (rc >= cc).astype(BF16)
    r2 = _iota2((2 * C, 2 * C), 0)
    c2 = _iota2((2 * C, 2 * C), 1)
    same = (r2 >= C) == (c2 >= C)
    strict = same & (r2 > c2)
    incl = same & (r2 >= c2)
    eye2 = (r2 == c2).astype(F32)

    def stack(x):
        return jnp.concatenate([x, x], axis=0)

    def stack_masked(x):
        return jnp.concatenate([jnp.where(m0, x, 0.0), jnp.where(m0, 0.0, x)], axis=0)

    def sel(z):
        return jnp.where(m0, z[:C], z[C:])

    pairs = []
    for bi in range(nb):
        rp = rp_ref[bi]
        prev = jnp.where(rowi == 0, last_ref[bi], pltpu.roll(rp, 1, 0))
        xr = rp + (prev - rp) * mu_ref[...]
        last_row = rp[C - 1:C, :]
        last_ref[bi] = last_row
        sho_ref[bi] = last_row

        r_all = xr[:, :W]
        kr = xr[:, W:2 * W]
        v_all = xr[:, 2 * W:3 * W]
        pwa = xr[:, 3 * W:3 * W + LANES]
        pg = xr[:, 3 * W + LANES:]
        wa = _mm(jnp.where(m0, jnp.tanh(pwa), pwa), w2a_ref[...])
        w_raw = vec[0:1] + wa[:, :W]
        a_all = jax.nn.sigmoid(vec[1:2] + wa[:, W:])
        lw_all = -jnp.exp(-_softplus(-w_raw) - 0.5)
        gate = _mm(jax.nn.sigmoid(pg), g2_ref[...])
        kkp = kr * vec[2:3]
        k2_all = kr * (1.0 + (a_all - 1.0) * vec[3:4])

        for p in range(RWKV_PAIRS):
            sl = slice(p * LANES, (p + 1) * LANES)
            r = r_all[:, sl]
            k2 = k2_all[:, sl]
            v = v_all[:, sl]
            lw = lw_all[:, sl]
            kk0 = kkp[:, sl]
            kk = kk0 * lax.rsqrt(_mm(kk0 * kk0, bd) + EPS)

            gcum = _mm_exact_lhs(tri, lw)
            e_in = jnp.exp(gcum)
            e_inv = jnp.exp(-gcum)
            abar = -kk * jnp.exp(gcum - lw)
            bbar = kk * a_all[:, sl] * e_inv
            kbar = k2 * e_inv
            rbar = r * e_in

            a_s = stack_masked(abar)
            r_s = stack_masked(rbar)
            b_c = stack(bbar)
            k_c = stack(kbar)
            pairs.append(dict(
                bi=bi, p=p, sl=sl, r=r, k2=k2, v=v, abar=abar, bbar=bbar, kbar=kbar, rbar=rbar,
                e_last=e_in[C - 1:C, :], gate=gate[:, sl],
                l_ab=jnp.where(strict, _mm_nt(a_s, b_c), 0.0), l_ak=jnp.where(strict, _mm_nt(a_s, k_c), 0.0),
                m_rb=jnp.where(incl, _mm_nt(r_s, b_c), 0.0), m_rk=jnp.where(incl, _mm_nt(r_s, k_c), 0.0)))

    invs = [eye2 + pr["l_ab"] for pr in pairs]
    pws = [pr["l_ab"] for pr in pairs]
    for _ in range(levels - 1):
        pws = [_mm_inv(pw, pw) for pw in pws]
        invs = [inv + _mm_inv(inv, pw) for inv, pw in zip(invs, pws)]

    inv_n = 1.0 / hd
    v_cs = [stack(pr["v"]) for pr in pairs]
    lakv = [_mm(pr["l_ak"], v_c) for pr, v_c in zip(pairs, v_cs)]
    mrkv = [_mm(pr["m_rk"], v_c) for pr, v_c in zip(pairs, v_cs)]
    bonus = [_mm(pr["r"] * pr["k2"] * vec[4:5, pr["sl"]], bd) * pr["v"] for pr in pairs]
    states = [s_ref[pr["bi"], pr["p"]] for pr in pairs]
    ars = [_mm_nt(jnp.concatenate([pr["abar"], pr["rbar"]], axis=0), s) for pr, s in zip(pairs, states)]
    rhss = [ar[:C] + sel(x) for ar, x in zip(ars, lakv)]
    us = [sel(_mm(inv, stack(rhs))) for inv, rhs in zip(invs, rhss)]
    ys = [ar[C:] + sel(_mm(pr["m_rb"], stack(u)) + x) for ar, pr, u, x in zip(ars, pairs, us, mrkv)]
    upds = [_mm_tn(jnp.concatenate([u, pr["v"]], axis=0), jnp.concatenate([pr["bbar"], pr["kbar"]], axis=0))
            for u, pr in zip(us, pairs)]
    for pr, s, upd in zip(pairs, states, upds):
        s_ref[pr["bi"], pr["p"]] = (s + jnp.where(bd_mask, upd, 0.0)) * pr["e_last"]
    ycs = [y - _mm(y, bd) * inv_n for y in ys]
    vrs = [_mm(yc * yc, bd) * inv_n for yc in ycs]
    for pr, yc, var, bon in zip(pairs, ycs, vrs, bonus):
        sl = pr["sl"]
        yn = yc * lax.rsqrt(var + RWKV_GN_EPS) * vec[5:6, sl] + vec[6:7, sl]
        o_ref[pr["bi"], :, sl] = (yn + bon) * pr["gate"]

    @pl.when(c == pl.num_programs(1) - 1)
    def _():
        so_ref[...] = s_ref[...]


def _rwkv(rp, shift, s0_pairs, mu, vecs, w2a, g2):
    b, t, _ = rp.shape
    chunk = min(64, t)
    nb = _seqs_per_step(b)
    assert t % chunk == 0 and chunk % SUBLANES == 0
    state = (nb, RWKV_PAIRS, LANES, LANES)
    return pl.pallas_call(
        functools.partial(_rwkv_kernel, chunk=chunk, levels=_levels(chunk), nb=nb),
        grid=(b // nb, t // chunk),
        in_specs=[
            pl.BlockSpec((nb, chunk, RWKV_IN), lambda i, j: (i, j, 0)),
            pl.BlockSpec((nb, 1, RWKV_IN), lambda i, j: (i, 0, 0)),
            pl.BlockSpec(state, lambda i, j: (i, 0, 0, 0)),
            pl.BlockSpec((1, RWKV_IN), lambda i, j: (0, 0)),
            pl.BlockSpec((SUBLANES, RWKV_WIDTH), lambda i, j: (0, 0)),
            pl.BlockSpec(w2a.shape, lambda i, j: (0, 0)),
            pl.BlockSpec(g2.shape, lambda i, j: (0, 0)),
        ],
        out_specs=[
            pl.BlockSpec((nb, chunk, RWKV_WIDTH), lambda i, j: (i, j, 0)),
            pl.BlockSpec(state, lambda i, j: (i, 0, 0, 0)),
            pl.BlockSpec((nb, 1, RWKV_IN), lambda i, j: (i, 0, 0)),
        ],
        out_shape=[
            jax.ShapeDtypeStruct((b, t, RWKV_WIDTH), F32),
            jax.ShapeDtypeStruct((b, RWKV_PAIRS, LANES, LANES), F32),
            jax.ShapeDtypeStruct((b, 1, RWKV_IN), F32),
        ],
        scratch_shapes=[
            pltpu.VMEM((nb, 1, RWKV_IN), F32),
            pltpu.VMEM(state, F32),
        ],
        compiler_params=_params("parallel", "arbitrary"), name="rwkv7",
    )(rp, shift.reshape(b, 1, RWKV_IN), s0_pairs, mu.reshape(1, RWKV_IN), vecs, w2a, g2)


def _sb_tile(z, r_later, u_bf, mask):
    (att,), (r_new,) = _sb_tiles([z], [r_later], u_bf, mask, chained=False)
    return att, r_new


def _sb_tiles(zs, r_in, u_bf, mask, chained):
    sps = [jnp.maximum(z, 0.0) + jnp.log(1.0 + jnp.exp(-jnp.abs(z))) for z in zs]
    if mask is not None:
        sps = [jnp.where(mask, sp, 0.0) for sp in sps]
    cums = [jnp.dot(sp.astype(BF16), u_bf, preferred_element_type=F32) for sp in sps]
    sums = [jnp.sum(sp, axis=-1, keepdims=True) for sp in sps]
    if chained:
        rs = [r_in[0]]
        for s in sums:
            rs.append(rs[-1] + s)
        r_tiles, r_out = rs[:-1], rs[-1:]
    else:
        r_tiles = r_in
        r_out = [r + s for r, s in zip(r_in, sums)]
    atts = [jnp.exp(z - cum - r) for z, cum, r in zip(zs, cums, r_tiles)]
    if mask is not None:
        atts = [jnp.where(mask, att, 0.0) for att in atts]
    return atts, r_out


def _sb_prompt_kernel(bias_ref, q_ref, k_ref, v_ref, o_ref, acc_ref, *, blk, kt, pg):
    g = pl.program_id(1)
    qi = pl.program_id(2)
    hd = SB_HEAD_DIM
    m0 = _iota2((blk, LANES), 1) < hd
    u_bf = (_iota2((kt, kt), 0) >= _iota2((kt, kt), 1)).astype(BF16)
    ktop = (qi * blk) // kt
    causal = (ktop * kt + _iota2((blk, kt), 1)) < (qi * blk + _iota2((blk, kt), 0))
    qs, bias = [], []
    for pp in range(pg):
        qp = q_ref[0, :, pp * LANES:(pp + 1) * LANES]
        zero = jnp.zeros_like(qp)
        qs.append((jnp.where(m0, qp, zero), jnp.where(m0, zero, qp)))
        head = 2 * (g * pg + pp)
        bias.append((bias_ref[head], bias_ref[head + 1]))
    acc_ref[...] = jnp.zeros_like(acc_ref)

    def tile(kj, r_later, mask):
        start = pl.multiple_of(kj * kt, kt)
        kbs = [k_ref[0, pl.ds(start, kt), pp * LANES:(pp + 1) * LANES] for pp in range(pg)]
        vbs = [v_ref[0, pl.ds(start, kt), pp * LANES:(pp + 1) * LANES] for pp in range(pg)]
        zs = [lax.dot_general(qs[pp][h], kbs[pp], (((1,), (1,)), ((), ())),
                              preferred_element_type=F32) + bias[pp][h]
              for pp in range(pg) for h in range(2)]
        atts, r_new = _sb_tiles(zs, list(r_later), u_bf, mask, chained=False)
        pvs = [jnp.dot(att.astype(BF16), vbs[i // 2], preferred_element_type=F32)
               for i, att in enumerate(atts)]
        for pp in range(pg):
            acc_ref[pp] += jnp.where(m0, pvs[2 * pp], pvs[2 * pp + 1])
        return tuple(r_new)

    r_later = tuple(jnp.zeros((blk, 1), F32) for _ in range(2 * pg))
    r_later = tile(ktop, r_later, causal)
    lax.fori_loop(0, ktop, lambda i, r: tile(ktop - 1 - i, r, None), r_later)
    for pp in range(pg):
        o_ref[0, :, pp * LANES:(pp + 1) * LANES] = acc_ref[pp]


def _sb_prompt(q, k, v, bias, pg=8):
    b, t, d = q.shape
    blk = 128 if t % 128 == 0 else t
    kt = 256 if t % 256 == 0 else blk
    assert blk % (2 * SUBLANES) == 0 and kt % blk == 0 and (d // LANES) % pg == 0
    w = pg * LANES
    return pl.pallas_call(
        functools.partial(_sb_prompt_kernel, blk=blk, kt=kt, pg=pg),
        grid=(b, d // w, t // blk),
        in_specs=[
            pl.BlockSpec(memory_space=pltpu.SMEM),
            pl.BlockSpec((1, blk, w), lambda i, p, j: (i, j, p)),
            pl.BlockSpec((1, t, w), lambda i, p, j: (i, 0, p)),
            pl.BlockSpec((1, t, w), lambda i, p, j: (i, 0, p)),
        ],
        out_specs=pl.BlockSpec((1, blk, w), lambda i, p, j: (i, j, p)),
        out_shape=jax.ShapeDtypeStruct((b, t, d), F32),
        scratch_shapes=[pltpu.VMEM((pg, blk, LANES), F32)],
        compiler_params=_params("parallel", "parallel", "arbitrary"), name="sb_prompt",
    )(bias, q, k, v)


def _sb_paged_kernel(pt_ref, bias_ref, q_ref, kn_ref, vn_ref, *refs, scale, pps, tq):
    k_refs = refs[:pps]
    v_refs = refs[pps:2 * pps]
    o_ref, qx_ref, acc_ref, r_ref = refs[2 * pps:]
    s = pl.program_id(1)
    hd = SB_HEAD_DIM
    rows = SB_HEADS * tq
    d = SB_HEADS * hd

    @pl.when(s == 0)
    def _():
        q = q_ref[0] * scale
        qt = jnp.concatenate([q] * SB_HEADS, axis=0)
        rh = _iota2((rows, d), 0) // tq
        lh = _iota2((rows, d), 1) // hd
        qx_ref[...] = jnp.where(rh == lh, qt, 0.0).astype(BF16)
        z = _mm_nt(qx_ref[...], kn_ref[0]) + bias_ref[:, :tq]
        rr = _iota2((rows, tq), 0)
        cc = _iota2((rows, tq), 1)
        mask = cc < (rr & (tq - 1))
        ur = _iota2((tq, tq), 0)
        uc = _iota2((tq, tq), 1)
        att, r_later = _sb_tile(z, jnp.zeros((rows, 1), F32), (ur >= uc).astype(BF16), mask)
        acc_ref[...] = _mm(att, vn_ref[0])
        r_ref[...] = jnp.broadcast_to(r_later, r_ref.shape)

    ur = _iota2((PAGE_SIZE, PAGE_SIZE), 0)
    uc = _iota2((PAGE_SIZE, PAGE_SIZE), 1)
    u_bf = (ur >= uc).astype(BF16)
    qx = qx_ref[...]
    zs = [_mm(qx, k_refs[j][0]) + bias_ref[...] for j in range(pps)]
    atts, (r_later,) = _sb_tiles(zs, [r_ref[...]], u_bf, None, chained=True)
    acc = acc_ref[...]
    for j in range(pps):
        acc = acc + _mm_nt(atts[j], v_refs[j][0])
    acc_ref[...] = acc
    r_ref[...] = r_later

    @pl.when(s == pl.num_programs(1) - 1)
    def _():
        lh = _iota2((tq, d), 1) // hd
        out = jnp.zeros((tq, d), F32)
        for h in range(SB_HEADS):
            out = out + jnp.where(lh == h, acc[h * tq:(h + 1) * tq, :], 0.0)
        o_ref[0] = out


def _sb_paged(q, k_new, v_new, pool_k, pool_v, page_table, pool_base, bias):
    b, tq, d = q.shape
    n_pages = page_table.shape[1]
    pps = PAGES_PER_STEP
    while n_pages % pps:
        pps //= 2
    assert tq & (tq - 1) == 0 and (SB_HEADS * tq) % SUBLANES == 0
    rows = SB_HEADS * tq
    bias_rows = jnp.broadcast_to(jnp.repeat(bias, tq)[:, None], (rows, PAGE_SIZE))

    def page_map(j):
        return lambda i, s, pt: (pool_base + pt[i, n_pages - 1 - (s * pps + j)], 0, 0)

    page_specs = [pl.BlockSpec((1, d, PAGE_SIZE), page_map(j)) for j in range(pps)]
    row_spec = pl.BlockSpec((1, tq, d), lambda i, s, pt: (i, 0, 0))
    grid_spec = pltpu.PrefetchScalarGridSpec(
        num_scalar_prefetch=1,
        grid=(b, n_pages // pps),
        in_specs=[pl.BlockSpec((rows, PAGE_SIZE), lambda i, s, pt: (0, 0)), row_spec, row_spec, row_spec]
        + page_specs + page_specs,
        out_specs=row_spec,
        scratch_shapes=[pltpu.VMEM((rows, d), BF16), pltpu.VMEM((rows, d), F32),
                        pltpu.VMEM((rows, PAGE_SIZE), F32)],
    )
    return pl.pallas_call(
        functools.partial(_sb_paged_kernel, scale=SB_HEAD_DIM ** -0.5, pps=pps, tq=tq),
        grid_spec=grid_spec,
        out_shape=jax.ShapeDtypeStruct((b, tq, d), F32),
        compiler_params=_params("parallel", "arbitrary"), name="sb_paged",
    )(page_table, bias_rows, q, k_new, v_new, *([pool_k] * pps), *([pool_v] * pps))


def _rwkv_state_to_pairs(s):
    b = s.shape[0]
    s = s.reshape(b, RWKV_PAIRS, 2, RWKV_HEAD_DIM, RWKV_HEAD_DIM)
    z = jnp.zeros_like(s[:, :, 0])
    top = jnp.concatenate([s[:, :, 0], z], axis=-1)
    bot = jnp.concatenate([z, s[:, :, 1]], axis=-1)
    return jnp.concatenate([top, bot], axis=-2)


def _rwkv_state_from_pairs(sp):
    b = sp.shape[0]
    hd = RWKV_HEAD_DIM
    return jnp.stack([sp[:, :, :hd, :hd], sp[:, :, hd:, hd:]], axis=2).reshape(b, 2 * RWKV_PAIRS, hd, hd)


def _prep_weights(p):
    w = {}
    ev = p["ev_w_in"][0]
    gdn_in = 4 * GDN_WIDTH
    w["ev_g"] = ev[:, :gdn_in].astype(BF16)
    w["ev_r"] = ev[:, gdn_in + 2 * GDN_HEADS:].astype(BF16)
    w["ev_bg"] = jnp.pad(ev[:, gdn_in:gdn_in + 2 * GDN_HEADS],
                         ((0, 0), (0, LANES - 2 * GDN_HEADS))).astype(BF16)
    w_out = p["ev_w_out"][0].astype(BF16)
    w["ev_out_a"], w["ev_out_b"] = w_out[:GDN_WIDTH], w_out[GDN_WIDTH:]
    hp = jnp.zeros((SUBLANES, LANES), F32)
    hp = hp.at[0, GDN_HEADS:2 * GDN_HEADS].set(p["gdn_a_log"][0])
    hp = hp.at[1, GDN_HEADS:2 * GDN_HEADS].set(p["gdn_dt_bias"][0])
    w["gdn_hp"] = hp
    names = ("rwkv_w0", "rwkv_a0", "rwkv_k_k", "rwkv_k_a", "rwkv_r_k", "rwkv_gn_g", "rwkv_gn_b")
    rows = [p[n][0] for n in names] + [jnp.zeros((RWKV_WIDTH,), F32)]
    w["rwkv_vecs"] = jnp.stack(rows)
    w2, a2 = p["rwkv_w2"][0], p["rwkv_a2"][0]
    zero = jnp.zeros_like(w2)
    w["rwkv_w2a"] = jnp.concatenate(
        [jnp.concatenate([w2, zero], axis=1), jnp.concatenate([zero, a2], axis=1)], axis=0).astype(BF16)
    w["rwkv_g2"] = p["rwkv_g2"][0].astype(BF16)
    sb = p["sb_w_in"][0].astype(BF16)
    d = sb.shape[0]
    w["sb_q"], w["sb_k"], w["sb_v"] = sb[:, :d], sb[:, d:2 * d], sb[:, 2 * d:]
    w["sb_out"] = p["sb_w_out"][0].astype(BF16)
    for n in ("mem_w_q", "mem_w_k", "mem_w_v", "mem_w_o", "ffn_w_in", "ffn_w_out"):
        w[n] = p[n].astype(BF16)
    return w


def _trunk(x, mem_k, mem_v, mem_stride, conv_buf, s_gdn, s_rwkv, shift, sb_past, p, w):
    b, t, d = x.shape
    m = b * t
    gp, rp, bg = _norm_proj(x.reshape(m, d), p["norm_mix"][0], [w["ev_g"], w["ev_r"], w["ev_bg"]])
    o_a, gdn_new, conv_new = _gdn(gp.reshape(b, t, -1), bg.reshape(b, t, LANES), conv_buf, s_gdn,
                                  p["gdn_conv_w"][0], w["gdn_hp"], p["gdn_norm"][0])
    o_b, rwkv_new, shift_new = _rwkv(rp.reshape(b, t, RWKV_IN), shift, _rwkv_state_to_pairs(s_rwkv),
                                     p["rwkv_mu"][0], w["rwkv_vecs"], w["rwkv_w2a"], w["rwkv_g2"])
    x3 = _mix_mem_attn(x, [o_a, o_b], [w["ev_out_a"], w["ev_out_b"]], p["norm_mem"][0],
                       w["mem_w_q"][0], w["mem_w_o"][0], mem_k, mem_v, 0)
    x4 = _ffn(x3.reshape(m, d), p["norm_ffn"][0], w["ffn_w_in"][0], w["ffn_w_out"][0],
              p["norm_final"], False)
    sb_w = [w["sb_q"], w["sb_k"], w["sb_v"]]
    heads = (b, t, SB_HEADS, SB_HEAD_DIM)
    if sb_past is None:
        outs = [(0, BF16, SB_HEAD_DIM ** -0.5, False), (1, BF16, 1.0, False), (2, BF16, 1.0, False),
                (1, F32, 1.0, True), (2, F32, 1.0, True)]
        q_bf, k_bf, v_bf, k_t, v_t = _norm_proj(x4, p["norm_mix"][1], sb_w, outs, seq_len=t)
        y = _sb_prompt(q_bf.reshape(b, t, d), k_bf.reshape(b, t, d), v_bf.reshape(b, t, d), p["sb_bias"][0])
        k_out, v_out = (jnp.transpose(a.reshape(b, SB_HEADS, SB_HEAD_DIM, t), (0, 3, 1, 2)) for a in (k_t, v_t))
    else:
        q, k, v = (a.reshape(b, t, d) for a in _norm_proj(x4, p["norm_mix"][1], sb_w))
        pool_k, pool_v, page_table = sb_past
        y = _sb_paged(q, k, v, pool_k, pool_v, page_table, 0, p["sb_bias"][0])
        k_out, v_out = k.reshape(heads), v.reshape(heads)
    x6 = _mix_mem_attn(x4.reshape(b, t, d), [y], [w["sb_out"]], p["norm_mem"][1],
                       w["mem_w_q"][1], w["mem_w_o"][1], mem_k, mem_v, mem_stride)
    y_out = _ffn(x6.reshape(m, d), p["norm_ffn"][1], w["ffn_w_in"][1], w["ffn_w_out"][1],
                 p["norm_final"], True).reshape(b, t, d)
    return (y_out, conv_new[None], gdn_new[None], _rwkv_state_from_pairs(rwkv_new)[None],
            shift_new.reshape(b, RWKV_IN)[None], k_out[None], v_out[None])


def kernel(x_prompt, x_sample, mem_prompt, state_gdn, state_gdn_conv, state_rwkv, state_rwkv_shift,
           cache_sb_k, cache_sb_v, cache_mem_k, cache_mem_v, page_table,
           norm_mix, norm_mem, norm_memtok, norm_ffn, norm_final, ev_w_in, ev_w_out,
           gdn_conv_w, gdn_a_log, gdn_dt_bias, gdn_norm, rwkv_mu, rwkv_w0, rwkv_w2, rwkv_a0, rwkv_a2, rwkv_g2,
           rwkv_k_k, rwkv_k_a, rwkv_r_k, rwkv_gn_g, rwkv_gn_b, sb_w_in, sb_w_out, sb_bias,
           mem_w_q, mem_w_k, mem_w_v, mem_w_o, ffn_w_in, ffn_w_out):
    p = dict(norm_mix=norm_mix, norm_mem=norm_mem, norm_memtok=norm_memtok, norm_ffn=norm_ffn,
             norm_final=norm_final, ev_w_in=ev_w_in, ev_w_out=ev_w_out, gdn_conv_w=gdn_conv_w,
             gdn_a_log=gdn_a_log, gdn_dt_bias=gdn_dt_bias, gdn_norm=gdn_norm, rwkv_mu=rwkv_mu,
             rwkv_w0=rwkv_w0, rwkv_w2=rwkv_w2, rwkv_a0=rwkv_a0, rwkv_a2=rwkv_a2, rwkv_g2=rwkv_g2,
             rwkv_k_k=rwkv_k_k, rwkv_k_a=rwkv_k_a, rwkv_r_k=rwkv_r_k, rwkv_gn_g=rwkv_gn_g,
             rwkv_gn_b=rwkv_gn_b, sb_w_in=sb_w_in, sb_w_out=sb_w_out, sb_bias=sb_bias,
             mem_w_q=mem_w_q, mem_w_k=mem_w_k, mem_w_v=mem_w_v, mem_w_o=mem_w_o,
             ffn_w_in=ffn_w_in, ffn_w_out=ffn_w_out)
    assert ev_w_in.shape[0] == 1 and sb_w_in.shape[0] == 1 and norm_mix.shape[0] == 2
    w = _prep_weights(p)
    bp, tp, d = x_prompt.shape
    bs = x_sample.shape[0]
    n_mem = mem_prompt.shape[1]

    mem2d = mem_prompt.reshape(bp * n_mem, d)
    mk_l, mv_l = [], []
    for layer in range(2):
        mk, mv = _norm_proj(mem2d, norm_memtok[layer], [w["mem_w_k"][layer], w["mem_w_v"][layer]])
        mk_l.append(mk.reshape(bp, n_mem, d))
        mv_l.append(mv.reshape(bp, n_mem, d))
    mem_k_p = jnp.stack(mk_l)
    mem_v_p = jnp.stack(mv_l)
    zeros = functools.partial(jnp.zeros, dtype=F32)
    out_p = _trunk(
        x_prompt, mem_k_p.reshape(2 * bp, n_mem, d), mem_v_p.reshape(2 * bp, n_mem, d), bp,
        zeros((bp, CONV_WIDTH - 1, GDN_CONV_CH)), zeros((bp, GDN_HEADS, GDN_HEAD_DIM, GDN_HEAD_DIM)),
        zeros((bp, 2 * RWKV_PAIRS, RWKV_HEAD_DIM, RWKV_HEAD_DIM)), zeros((bp, RWKV_IN)), None, p, w)

    n_pool = cache_sb_k.shape[1]
    pool_k = jnp.transpose(cache_sb_k, (0, 1, 3, 4, 2)).reshape(cache_sb_k.shape[0] * n_pool, d, PAGE_SIZE)
    pool_v = jnp.transpose(cache_sb_v, (0, 1, 3, 4, 2)).reshape(cache_sb_v.shape[0] * n_pool, d, PAGE_SIZE)
    out_s = _trunk(
        x_sample, cache_mem_k.reshape(2 * bs, n_mem, d), cache_mem_v.reshape(2 * bs, n_mem, d), bs,
        state_gdn_conv[0], state_gdn[0], state_rwkv[0], state_rwkv_shift[0],
        (pool_k, pool_v, page_table), p, w)

    y_p, conv_p, gdn_p, rwkv_p, shift_p, sbk_p, sbv_p = out_p
    y_s, conv_s, gdn_s, rwkv_s, shift_s, sbk_s, sbv_s = out_s
    mem_shape = (2, bp, n_mem, MEM_HEADS, d // MEM_HEADS)
    return (y_p, y_s, gdn_p, gdn_s, conv_p, conv_s, rwkv_p, rwkv_s, shift_p, shift_s,
            sbk_p, sbk_s, sbv_p, sbv_s, mem_k_p.reshape(mem_shape), mem_v_p.reshape(mem_shape))
```

```python
import functools
import math

import jax
import jax.numpy as jnp
from jax import lax
from jax.experimental import pallas as pl
from jax.experimental.pallas import tpu as pltpu

F32 = jnp.float32
BF16 = jnp.bfloat16

EPS = 1e-6
RWKV_GN_EPS = 64e-5
LOG2E = math.log2(math.e)
CONV_WIDTH = 4
GDN_HEADS = 4
GDN_HEAD_DIM = 128
GDN_WIDTH = GDN_HEADS * GDN_HEAD_DIM
GDN_CONV_CH = 3 * GDN_WIDTH
RWKV_HEAD_DIM = 64
RWKV_WIDTH = 512
RWKV_PAIRS = RWKV_WIDTH // 128
RWKV_IN = 1792
SB_HEADS = 16
SB_HEAD_DIM = 64
MEM_HEADS = 4
PAGE_SIZE = 128
LANES = 128
SUBLANES = 8
V7X_VMEM_LIMIT_BYTES = 56 * 1024 * 1024
PAGES_PER_STEP = 16


def _params(*sem):
    return pltpu.CompilerParams(dimension_semantics=sem, vmem_limit_bytes=V7X_VMEM_LIMIT_BYTES)


def _mm(a, b):
    return jnp.dot(a.astype(BF16), b.astype(BF16), preferred_element_type=F32)


def _mm_nt(a, b):
    return lax.dot_general(a.astype(BF16), b.astype(BF16), (((1,), (1,)), ((), ())),
                           preferred_element_type=F32)


def _mm_tn(a, b):
    return lax.dot_general(a.astype(BF16), b.astype(BF16), (((0,), (0,)), ((), ())),
                           preferred_element_type=F32)


def _hi_lo(x):
    hi = x.astype(BF16)
    lo = (x - hi.astype(F32)).astype(BF16)
    return hi, lo


def _mm3(a, b):
    ah, al = _hi_lo(a)
    bh, bl = _hi_lo(b)
    d = functools.partial(jnp.dot, preferred_element_type=F32)
    return d(ah, bh) + d(ah, bl) + d(al, bh)


INVERSE_PASSES = 1


def _mm_inv(a, b):
    return _mm3(a, b) if INVERSE_PASSES == 3 else _mm(a, b)


def _mm_exact_lhs(a_bf, x):
    xh, xl = _hi_lo(x)
    return (jnp.dot(a_bf, xh, preferred_element_type=F32)
            + jnp.dot(a_bf, xl, preferred_element_type=F32))


def _rms(x, g):
    return x * lax.rsqrt(jnp.mean(x * x, axis=-1, keepdims=True) + EPS) * g


def _softplus(x):
    return jnp.maximum(x, 0.0) + jnp.log1p(jnp.exp(-jnp.abs(x)))


def _iota2(shape, dim):
    return lax.broadcasted_iota(jnp.int32, shape, dim)


def _row_tile(m, want):
    t = min(m, want)
    while m % t:
        t //= 2
    return t


def _levels(c):
    return max(1, int(math.ceil(math.log2(c))))


def _norm_proj_kernel(x_ref, g_ref, *refs, n_w, outs):
    xn = _rms(x_ref[...], g_ref[...]).astype(BF16)
    prods = [jnp.dot(xn, w_ref[...], preferred_element_type=F32) for w_ref in refs[:n_w]]
    for (wi, dtype, scale, transposed), o_ref in zip(outs, refs[n_w:]):
        y = (prods[wi] if scale == 1.0 else prods[wi] * scale).astype(dtype)
        if transposed == "head_tiles":
            rows, n = y.shape
            hd = n // MEM_HEADS
            for c in range(SUBLANES):
                h, part = c % MEM_HEADS, c // MEM_HEADS
                lo = h * hd + part * LANES
                o_ref[pl.ds(c, rows, stride=SUBLANES), :] = y[:, lo:lo + LANES]
        elif transposed:
            o_ref[0] = y.T
        else:
            o_ref[...] = y


def _norm_proj(x2d, g, ws, outs=None, tm=512, seq_len=None):
    m, d = x2d.shape
    tm = _row_tile(seq_len if seq_len else m, tm)
    if outs is None:
        outs = [(i, F32, 1.0, False) for i in range(len(ws))]
    in_specs = [pl.BlockSpec((tm, d), lambda i: (i, 0)), pl.BlockSpec((1, d), lambda i: (0, 0))]
    in_specs += [pl.BlockSpec(w.shape, lambda i: (0, 0)) for w in ws]
    out_specs, out_shape = [], []
    for wi, dt, _, transposed in outs:
        n = ws[wi].shape[1]
        if transposed == "head_tiles":
            assert n == MEM_HEADS * 2 * LANES
            out_specs.append(pl.BlockSpec((tm * SUBLANES, LANES), lambda i: (i, 0)))
            out_shape.append(jax.ShapeDtypeStruct((m * SUBLANES, LANES), dt))
        elif transposed:
            nt = seq_len // tm
            out_specs.append(pl.BlockSpec((1, n, tm), lambda i, nt=nt: (i // nt, 0, i % nt)))
            out_shape.append(jax.ShapeDtypeStruct((m // seq_len, n, seq_len), dt))
        else:
            out_specs.append(pl.BlockSpec((tm, n), lambda i: (i, 0)))
            out_shape.append(jax.ShapeDtypeStruct((m, n), dt))
    return pl.pallas_call(
        functools.partial(_norm_proj_kernel, n_w=len(ws), outs=tuple(outs)),
        grid=(m // tm,), in_specs=in_specs, out_specs=out_specs, out_shape=out_shape,
        compiler_params=_params("parallel"), name="norm_proj",
    )(x2d, g.reshape(1, d), *ws)


def _ffn_kernel(x_ref, g_ref, wg_ref, wu_ref, wo_ref, gf_ref, o_ref, xn_ref, acc_ref, *, final_norm):
    j = pl.program_id(1)

    @pl.when(j == 0)
    def _():
        xn_ref[...] = _rms(x_ref[...], g_ref[...]).astype(BF16)
        acc_ref[...] = jnp.zeros_like(acc_ref)

    xn = xn_ref[...]
    gt = jnp.dot(xn, wg_ref[...], preferred_element_type=F32)
    ut = jnp.dot(xn, wu_ref[...], preferred_element_type=F32)
    act = (gt * jax.nn.sigmoid(gt) * ut).astype(BF16)
    acc_ref[...] += jnp.dot(act, wo_ref[...], preferred_element_type=F32)

    @pl.when(j == pl.num_programs(1) - 1)
    def _():
        y = x_ref[...] + acc_ref[...]
        if final_norm:
            y = _rms(y, gf_ref[...])
        o_ref[...] = y


def _ffn(x2d, g, w_in, w_out, g_final, final_norm, tm=512, n_chunks=2):
    m, d = x2d.shape
    dff = w_out.shape[0]
    tf = dff // n_chunks
    tm = _row_tile(m, tm)
    return pl.pallas_call(
        functools.partial(_ffn_kernel, final_norm=final_norm),
        grid=(m // tm, n_chunks),
        in_specs=[
            pl.BlockSpec((tm, d), lambda i, j: (i, 0)),
            pl.BlockSpec((1, d), lambda i, j: (0, 0)),
            pl.BlockSpec((d, tf), lambda i, j: (0, j)),
            pl.BlockSpec((d, tf), lambda i, j: (0, n_chunks + j)),
            pl.BlockSpec((tf, d), lambda i, j: (j, 0)),
            pl.BlockSpec((1, d), lambda i, j: (0, 0)),
        ],
        out_specs=pl.BlockSpec((tm, d), lambda i, j: (i, 0)),
        out_shape=jax.ShapeDtypeStruct((m, d), F32),
        scratch_shapes=[pltpu.VMEM((tm, d), BF16), pltpu.VMEM((tm, d), F32)],
        compiler_params=_params("parallel", "arbitrary"), name="ffn",
    )(x2d, g.reshape(1, d), w_in, w_in, w_out, g_final.reshape(1, d))


def _mix_mem_attn_kernel(x_ref, *refs, scale, heads, n_in):
    act_refs, w_refs = refs[:n_in], refs[n_in:2 * n_in]
    g_ref, wq_ref, wo_ref, mk_ref, mv_ref, o_ref = refs[2 * n_in:]
    ns, tm, d = x_ref.shape
    x = x_ref[...].reshape(ns * tm, d)
    for a_ref, w_ref in zip(act_refs, w_refs):
        a = a_ref[...].reshape(ns * tm, a_ref.shape[2])
        x = x + jnp.dot(a.astype(BF16), w_ref[...], preferred_element_type=F32)
    xn = _rms(x, g_ref[...]).astype(BF16)
    q = (jnp.dot(xn, wq_ref[...], preferred_element_type=F32) * scale).astype(BF16)
    hd = d // heads
    units = [(s, h, slice(h * hd, (h + 1) * hd)) for s in range(ns) for h in range(heads)]
    mks = [mk_ref[s].astype(BF16) for s in range(ns)]
    mvs = [mv_ref[s].astype(BF16) for s in range(ns)]
    scores = [_mm_nt(q[s * tm:(s + 1) * tm, sl], mks[s][:, sl]) for s, _, sl in units]
    es = [jnp.exp(sc - jnp.max(sc, axis=-1, keepdims=True)) for sc in scores]
    pvs = [_mm(e, mvs[s][:, sl]) / jnp.sum(e, axis=-1, keepdims=True) for e, (s, _, sl) in zip(es, units)]
    o = jnp.concatenate([jnp.concatenate(pvs[s * heads:(s + 1) * heads], axis=-1) for s in range(ns)], axis=0)
    y = x + jnp.dot(o.astype(BF16), wo_ref[...], preferred_element_type=F32)
    o_ref[...] = y.reshape(ns, tm, d)


def _mix_mem_attn(x, acts, ws, g, wq, wo, mk, mv, mem_base, tm=512):
    b, t, d = x.shape
    tm = _row_tile(t, tm)
    ns = 1
    if tm == t:
        while ns < 4 and b % (2 * ns) == 0 and mem_base % (2 * ns) == 0 and 2 * ns * tm <= 512:
            ns *= 2
    n_mem = mk.shape[1]
    seq = lambda width: pl.BlockSpec((ns, tm, width), lambda i, j: (i, j, 0))
    const = lambda shape: pl.BlockSpec(shape, lambda i, j: (0,) * len(shape))
    mem = pl.BlockSpec((ns, n_mem, d), lambda i, j: (mem_base // ns + i, 0, 0))
    return pl.pallas_call(
        functools.partial(_mix_mem_attn_kernel, scale=(d // MEM_HEADS) ** -0.5, heads=MEM_HEADS,
                          n_in=len(acts)),
        grid=(b // ns, t // tm),
        in_specs=[seq(d)] + [seq(a.shape[2]) for a in acts] + [const(w_.shape) for w_ in ws]
        + [const((1, d)), const((d, d)), const((d, d)), mem, mem],
        out_specs=seq(d),
        out_shape=jax.ShapeDtypeStruct((b, t, d), F32),
        compiler_params=_params("parallel", "parallel"), name="mix_mem_attn",
    )(x, *acts, *ws, g.reshape(1, d), wq, wo, mk, mv)


def _gdn_kernel(gp_ref, bg_ref, cb_ref, s0_ref, cw_ref, hp_ref, gn_ref, o_ref, so_ref, cv_ref,
                ext_ref, s_ref, *, chunk, levels, nb):
    c = pl.program_id(1)
    C = chunk
    dk = GDN_HEAD_DIM
    tail = CONV_WIDTH - 1

    @pl.when(c == 0)
    def _():
        ext_ref[:, SUBLANES - tail:SUBLANES, :] = cb_ref[...]
        s_ref[...] = s0_ref[...]

    C2 = 2 * C
    r2 = _iota2((C2, C2), 0)
    c2 = _iota2((C2, C2), 1)
    same = (r2 >= C) == (c2 >= C)
    lower = same & (r2 >= c2)
    strict = same & (r2 > c2)
    tri = lower.astype(BF16)
    block_ones = same.astype(BF16)
    eye = (r2 == c2).astype(F32)
    first = _iota2((C2, dk), 0) < C
    lane = _iota2((C, LANES), 1)
    cw = cw_ref[...]

    groups = []
    for bi in range(nb):
        gp = gp_ref[bi]
        u_in = gp[:, :GDN_CONV_CH]
        ext_ref[bi, SUBLANES:SUBLANES + C, :] = u_in
        conv = u_in * cw[tail:tail + 1]
        for j in range(tail):
            lo = SUBLANES - tail + j
            conv = conv + ext_ref[bi, lo:lo + C, :] * cw[j:j + 1]
        new_tail = ext_ref[bi, SUBLANES + C - tail:SUBLANES + C, :]
        ext_ref[bi, SUBLANES - tail:SUBLANES, :] = new_tail
        cv_ref[bi] = new_tail
        qkv = conv * jax.nn.sigmoid(conv)

        bg = bg_ref[bi]
        beta_all = jax.nn.sigmoid(bg)
        g_all = -jnp.exp(hp_ref[0:1, :]) * _softplus(bg + hp_ref[1:2, :])

        def head(h, qkv=qkv, beta_all=beta_all, g_all=g_all):
            qh = qkv[:, h * dk:(h + 1) * dk]
            kh = qkv[:, GDN_WIDTH + h * dk:GDN_WIDTH + (h + 1) * dk]
            vh = qkv[:, 2 * GDN_WIDTH + h * dk:2 * GDN_WIDTH + (h + 1) * dk]
            qn = qh * lax.rsqrt(jnp.sum(qh * qh, axis=-1, keepdims=True) + EPS) * (dk ** -0.5)
            kn = kh * lax.rsqrt(jnp.sum(kh * kh, axis=-1, keepdims=True) + EPS)
            beta = jnp.sum(jnp.where(lane == h, beta_all, 0.0), axis=-1, keepdims=True)
            g = jnp.sum(jnp.where(lane == GDN_HEADS + h, g_all, 0.0), axis=-1, keepdims=True)
            return qn, kn, vh, beta, g

        for gi in range(GDN_HEADS // 2):
            h0, h1 = head(2 * gi), head(2 * gi + 1)
            qs, ks, vs, beta, g = (jnp.concatenate([a, b], axis=0) for a, b in zip(h0, h1))
            gcum = _mm_exact_lhs(tri, jnp.broadcast_to(g, (C2, dk)))
            g_cc = jnp.broadcast_to(g, (C2, C2))
            gcum_i = _mm_exact_lhs(tri, g_cc)
            gcum_j = _mm_exact_lhs(block_ones, jnp.where(same & (r2 <= c2), g_cc, 0.0))
            decay = jnp.where(lower, jnp.exp(jnp.minimum(gcum_i - gcum_j, 0.0)), 0.0)
            gam = jnp.exp(gcum)
            glast = jnp.where(first, gcum[C - 1:C, :], gcum[C2 - 1:C2, :])
            x = jnp.where(strict, -(beta * _mm_nt(ks, ks) * decay), 0.0)
            groups.append(dict(bi=bi, gi=gi, gp=gp, qs=qs, ks=ks, vs=vs, beta=beta, gam=gam, glast=glast,
                               gcum=gcum, x=x, qkd=_mm_nt(qs, ks) * decay))

    invs = [eye + gr["x"] for gr in groups]
    pws = [gr["x"] for gr in groups]
    for _ in range(levels - 1):
        pws = [_mm_inv(pw, pw) for pw in pws]
        invs = [inv + _mm_inv(inv, pw) for inv, pw in zip(invs, pws)]

    heads = [(n, j, slice(j * C, (j + 1) * C)) for n in range(len(groups)) for j in range(2)]
    w_mats = [_mm(inv, gr["beta"] * gr["gam"] * gr["ks"]) for gr, inv in zip(groups, invs)]
    u0s = [_mm(inv, gr["beta"] * gr["vs"]) for gr, inv in zip(groups, invs)]
    kdecs = [gr["ks"] * jnp.exp(gr["glast"] - gr["gcum"]) for gr in groups]
    states = [s_ref[groups[n]["bi"], 2 * groups[n]["gi"] + j] for n, j, _ in heads]
    ws_s = [_mm(w_mats[n][rows], s) for (n, _, rows), s in zip(heads, states)]
    qs_s = [_mm(groups[n]["qs"][rows], s) for (n, _, rows), s in zip(heads, states)]
    us = [u0s[n][rows] - ws for (n, _, rows), ws in zip(heads, ws_s)]
    u_g = [jnp.concatenate(us[2 * n:2 * n + 2], axis=0) for n in range(len(groups))]
    o_g = [gr["gam"] * jnp.concatenate(qs_s[2 * n:2 * n + 2], axis=0) + _mm(gr["qkd"], u_g[n])
           for n, gr in enumerate(groups)]
    upds = [_mm_tn(kdecs[n][rows], u) for (n, _, rows), u in zip(heads, us)]
    for (n, j, rows), s, upd in zip(heads, states, upds):
        gr = groups[n]
        bi, h = gr["bi"], 2 * gr["gi"] + j
        s_ref[bi, h] = jnp.exp(gr["glast"][j * C:j * C + 1, :]) * s + upd
        zh = gr["gp"][:, GDN_CONV_CH + h * dk:GDN_CONV_CH + (h + 1) * dk]
        o_ref[bi, :, h * dk:(h + 1) * dk] = _rms(o_g[n][rows], gn_ref[...]) * (zh * jax.nn.sigmoid(zh))

    @pl.when(c == pl.num_programs(1) - 1)
    def _():
        so_ref[...] = s_ref[...]


def _seqs_per_step(b):
    for nb in (8, 4, 2):
        if b % nb == 0:
            return nb
    return 1


def _gdn(gp, bg, conv_buf, s0, conv_w, head_params, gnorm):
    b, t, _ = gp.shape
    chunk = min(64, t)
    nb = _seqs_per_step(b)
    assert t % chunk == 0 and chunk % SUBLANES == 0 and chunk >= CONV_WIDTH - 1
    state = (nb, GDN_HEADS, GDN_HEAD_DIM, GDN_HEAD_DIM)
    return pl.pallas_call(
        functools.partial(_gdn_kernel, chunk=chunk, levels=_levels(chunk), nb=nb),
        grid=(b // nb, t // chunk),
        in_specs=[
            pl.BlockSpec((nb, chunk, gp.shape[2]), lambda i, j: (i, j, 0)),
            pl.BlockSpec((nb, chunk, LANES), lambda i, j: (i, j, 0)),
            pl.BlockSpec((nb, CONV_WIDTH - 1, GDN_CONV_CH), lambda i, j: (i, 0, 0)),
            pl.BlockSpec(state, lambda i, j: (i, 0, 0, 0)),
            pl.BlockSpec((CONV_WIDTH, GDN_CONV_CH), lambda i, j: (0, 0)),
            pl.BlockSpec((SUBLANES, LANES), lambda i, j: (0, 0)),
            pl.BlockSpec((1, GDN_HEAD_DIM), lambda i, j: (0, 0)),
        ],
        out_specs=[
            pl.BlockSpec((nb, chunk, GDN_WIDTH), lambda i, j: (i, j, 0)),
            pl.BlockSpec(state, lambda i, j: (i, 0, 0, 0)),
            pl.BlockSpec((nb, CONV_WIDTH - 1, GDN_CONV_CH), lambda i, j: (i, 0, 0)),
        ],
        out_shape=[
            jax.ShapeDtypeStruct((b, t, GDN_WIDTH), F32),
            jax.ShapeDtypeStruct((b, GDN_HEADS, GDN_HEAD_DIM, GDN_HEAD_DIM), F32),
            jax.ShapeDtypeStruct((b, CONV_WIDTH - 1, GDN_CONV_CH), F32),
        ],
        scratch_shapes=[
            pltpu.VMEM((nb, chunk + SUBLANES, GDN_CONV_CH), F32),
            pltpu.VMEM(state, F32),
        ],
        compiler_params=_params("parallel", "arbitrary"), name="gdn",
    )(gp, bg, conv_buf, s0, conv_w, head_params, gnorm.reshape(1, GDN_HEAD_DIM))


def _rwkv_kernel(rp_ref, sh_ref, s0_ref, mu_ref, vec_ref, w2a_ref, g2_ref, o_ref, so_ref, sho_ref,
                 last_ref, s_ref, *, chunk, levels, nb):
    c = pl.program_id(1)
    C = chunk
    W = RWKV_WIDTH
    hd = RWKV_HEAD_DIM

    @pl.when(c == 0)
    def _():
        last_ref[...] = sh_ref[...]
        s_ref[...] = s0_ref[...]

    vec = vec_ref[...]
    lane = _iota2((C, LANES), 1)
    m0 = lane < hd
    rowi = _iota2((C, RWKV_IN), 0)

    r128 = _iota2((LANES, LANES), 0)
    c128 = _iota2((LANES, LANES), 1)
    bd_mask = (r128 >= hd) == (c128 >= hd)
    bd = bd_mask.astype(BF16)
    rc = _iota2((C, C), 0)
    cc = _iota2((C, C), 1)
    tri = (rc >= cc).astype(BF16)
    r2 = _iota2((2 * C, 2 * C), 0)
    c2 = _iota2((2 * C, 2 * C), 1)
    same = (r2 >= C) == (c2 >= C)
    strict = same & (r2 > c2)
    incl = same & (r2 >= c2)
    eye2 = (r2 == c2).astype(F32)

    def stack(x):
        return jnp.concatenate([x, x], axis=0)

    def stack_masked(x):
        return jnp.concatenate([jnp.where(m0, x, 0.0), jnp.where(m0, 0.0, x)], axis=0)

    def sel(z):
        return jnp.where(m0, z[:C], z[C:])

    pairs = []
    for bi in range(nb):
        rp = rp_ref[bi]
        prev = jnp.where(rowi == 0, last_ref[bi], pltpu.roll(rp, 1, 0))
        xr = rp + (prev - rp) * mu_ref[...]
        last_row = rp[C - 1:C, :]
        last_ref[bi] = last_row
        sho_ref[bi] = last_row

        r_all = xr[:, :W]
        kr = xr[:, W:2 * W]
        v_all = xr[:, 2 * W:3 * W]
        pwa = xr[:, 3 * W:3 * W + LANES]
        pg = xr[:, 3 * W + LANES:]
        wa = _mm(jnp.where(m0, jnp.tanh(pwa), pwa), w2a_ref[...])
        w_raw = vec[0:1] + wa[:, :W]
        a_all = jax.nn.sigmoid(vec[1:2] + wa[:, W:])
        lw_all = -jnp.exp(-_softplus(-w_raw) - 0.5)
        gate = _mm(jax.nn.sigmoid(pg), g2_ref[...])
        kkp = kr * vec[2:3]
        k2_all = kr * (1.0 + (a_all - 1.0) * vec[3:4])

        for p in range(RWKV_PAIRS):
            sl = slice(p * LANES, (p + 1) * LANES)
            r = r_all[:, sl]
            k2 = k2_all[:, sl]
            v = v_all[:, sl]
            lw = lw_all[:, sl]
            kk0 = kkp[:, sl]
            kk = kk0 * lax.rsqrt(_mm(kk0 * kk0, bd) + EPS)

            gcum = _mm_exact_lhs(tri, lw)
            e_in = jnp.exp(gcum)
            e_inv = jnp.exp(-gcum)
            abar = -kk * jnp.exp(gcum - lw)
            bbar = kk * a_all[:, sl] * e_inv
            kbar = k2 * e_inv
            rbar = r * e_in

            a_s = stack_masked(abar)
            r_s = stack_masked(rbar)
            b_c = stack(bbar)
            k_c = stack(kbar)
            pairs.append(dict(
                bi=bi, p=p, sl=sl, r=r, k2=k2, v=v, abar=abar, bbar=bbar, kbar=kbar, rbar=rbar,
                e_last=e_in[C - 1:C, :], gate=gate[:, sl],
                l_ab=jnp.where(strict, _mm_nt(a_s, b_c), 0.0), l_ak=jnp.where(strict, _mm_nt(a_s, k_c), 0.0),
                m_rb=jnp.where(incl, _mm_nt(r_s, b_c), 0.0), m_rk=jnp.where(incl, _mm_nt(r_s, k_c), 0.0)))

    invs = [eye2 + pr["l_ab"] for pr in pairs]
    pws = [pr["l_ab"] for pr in pairs]
    for _ in range(levels - 1):
        pws = [_mm_inv(pw, pw) for pw in pws]
        invs = [inv + _mm_inv(inv, pw) for inv, pw in zip(invs, pws)]

    inv_n = 1.0 / hd
    v_cs = [stack(pr["v"]) for pr in pairs]
    lakv = [_mm(pr["l_ak"], v_c) for pr, v_c in zip(pairs, v_cs)]
    mrkv = [_mm(pr["m_rk"], v_c) for pr, v_c in zip(pairs, v_cs)]
    bonus = [_mm(pr["r"] * pr["k2"] * vec[4:5, pr["sl"]], bd) * pr["v"] for pr in pairs]
    states = [s_ref[pr["bi"], pr["p"]] for pr in pairs]
    ars = [_mm_nt(jnp.concatenate([pr["abar"], pr["rbar"]], axis=0), s) for pr, s in zip(pairs, states)]
    rhss = [ar[:C] + sel(x) for ar, x in zip(ars, lakv)]
    us = [sel(_mm(inv, stack(rhs))) for inv, rhs in zip(invs, rhss)]
    ys = [ar[C:] + sel(_mm(pr["m_rb"], stack(u)) + x) for ar, pr, u, x in zip(ars, pairs, us, mrkv)]
    upds = [_mm_tn(jnp.concatenate([u, pr["v"]], axis=0), jnp.concatenate([pr["bbar"], pr["kbar"]], axis=0))
            for u, pr in zip(us, pairs)]
    for pr, s, upd in zip(pairs, states, upds):
        s_ref[pr["bi"], pr["p"]] = (s + jnp.where(bd_mask, upd, 0.0)) * pr["e_last"]
    ycs = [y - _mm(y, bd) * inv_n for y in ys]
    vrs = [_mm(yc * yc, bd) * inv_n for yc in ycs]
    for pr, yc, var, bon in zip(pairs, ycs, vrs, bonus):
        sl = pr["sl"]
        yn = yc * lax.rsqrt(var + RWKV_GN_EPS) * vec[5:6, sl] + vec[6:7, sl]
        o_ref[pr["bi"], :, sl] = (yn + bon) * pr["gate"]

    @pl.when(c == pl.num_programs(1) - 1)
    def _():
        so_ref[...] = s_ref[...]


def _rwkv(rp, shift, s0_pairs, mu, vecs, w2a, g2):
    b, t, _ = rp.shape
    chunk = min(64, t)
    nb = _seqs_per_step(b)
    assert t % chunk == 0 and chunk % SUBLANES == 0
    state = (nb, RWKV_PAIRS, LANES, LANES)
    return pl.pallas_call(
        functools.partial(_rwkv_kernel, chunk=chunk, levels=_levels(chunk), nb=nb),
        grid=(b // nb, t // chunk),
        in_specs=[
            pl.BlockSpec((nb, chunk, RWKV_IN), lambda i, j: (i, j, 0)),
            pl.BlockSpec((nb, 1, RWKV_IN), lambda i, j: (i, 0, 0)),
            pl.BlockSpec(state, lambda i, j: (i, 0, 0, 0)),
            pl.BlockSpec((1, RWKV_IN), lambda i, j: (0, 0)),
            pl.BlockSpec((SUBLANES, RWKV_WIDTH), lambda i, j: (0, 0)),
            pl.BlockSpec(w2a.shape, lambda i, j: (0, 0)),
            pl.BlockSpec(g2.shape, lambda i, j: (0, 0)),
        ],
        out_specs=[
            pl.BlockSpec((nb, chunk, RWKV_WIDTH), lambda i, j: (i, j, 0)),
            pl.BlockSpec(state, lambda i, j: (i, 0, 0, 0)),
            pl.BlockSpec((nb, 1, RWKV_IN), lambda i, j: (i, 0, 0)),
        ],
        out_shape=[
            jax.ShapeDtypeStruct((b, t, RWKV_WIDTH), F32),
            jax.ShapeDtypeStruct((b, RWKV_PAIRS, LANES, LANES), F32),
            jax.ShapeDtypeStruct((b, 1, RWKV_IN), F32),
        ],
        scratch_shapes=[
            pltpu.VMEM((nb, 1, RWKV_IN), F32),
            pltpu.VMEM(state, F32),
        ],
        compiler_params=_params("parallel", "arbitrary"), name="rwkv7",
    )(rp, shift.reshape(b, 1, RWKV_IN), s0_pairs, mu.reshape(1, RWKV_IN), vecs, w2a, g2)


def _sb_tile(z, r_later, u_bf, mask):
    (att,), (r_new,) = _sb_tiles([z], [r_later], u_bf, mask, chained=False)
    return att, r_new


def _sb_tiles(zs, r_in, u_bf, mask, chained):
    sps = [jnp.maximum(z, 0.0) + jnp.log(1.0 + jnp.exp2(jnp.abs(z) * (-LOG2E))) for z in zs]
    if mask is not None:
        sps = [jnp.where(mask, sp, 0.0) for sp in sps]
    cums = [jnp.dot(sp.astype(BF16), u_bf, preferred_element_type=F32) for sp in sps]
    sums = [jnp.sum(sp, axis=-1, keepdims=True) for sp in sps]
    if chained:
        rs = [r_in[0]]
        for s in sums:
            rs.append(rs[-1] + s)
        r_tiles, r_out = rs[:-1], rs[-1:]
    else:
        r_tiles = r_in
        r_out = [r + s for r, s in zip(r_in, sums)]
    atts = [jnp.exp(z - cum - r) for z, cum, r in zip(zs, cums, r_tiles)]
    if mask is not None:
        atts = [jnp.where(mask, att, 0.0) for att in atts]
    return atts, r_out


def _sb_prompt_kernel(bias_ref, q_ref, k_ref, v_ref, o_ref, acc_ref, *, blk, kt, pg):
    g = pl.program_id(1)
    qi = pl.program_id(2)
    hd = SB_HEAD_DIM
    m0 = _iota2((blk, LANES), 1) < hd
    u_bf = (_iota2((kt, kt), 0) >= _iota2((kt, kt), 1)).astype(BF16)
    ktop = (qi * blk) // kt
    causal = (ktop * kt + _iota2((blk, kt), 1)) < (qi * blk + _iota2((blk, kt), 0))
    qs, bias = [], []
    for pp in range(pg):
        qp = q_ref[0, :, pp * LANES:(pp + 1) * LANES]
        zero = jnp.zeros_like(qp)
        qs.append((jnp.where(m0, qp, zero), jnp.where(m0, zero, qp)))
        head = 2 * (g * pg + pp)
        bias.append((bias_ref[head], bias_ref[head + 1]))
    acc_ref[...] = jnp.zeros_like(acc_ref)

    def tile(kj, r_later, mask):
        start = pl.multiple_of(kj * kt, kt)
        kbs = [k_ref[0, pl.ds(start, kt), pp * LANES:(pp + 1) * LANES] for pp in range(pg)]
        vbs = [v_ref[0, pl.ds(start, kt), pp * LANES:(pp + 1) * LANES] for pp in range(pg)]
        zs = [lax.dot_general(qs[pp][h], kbs[pp], (((1,), (1,)), ((), ())),
                              preferred_element_type=F32) + bias[pp][h]
              for pp in range(pg) for h in range(2)]
        atts, r_new = _sb_tiles(zs, list(r_later), u_bf, mask, chained=False)
        pvs = [jnp.dot(att.astype(BF16), vbs[i // 2], preferred_element_type=F32)
               for i, att in enumerate(atts)]
        for pp in range(pg):
            acc_ref[pp] += jnp.where(m0, pvs[2 * pp], pvs[2 * pp + 1])
        return tuple(r_new)

    r_later = tuple(jnp.zeros((blk, 1), F32) for _ in range(2 * pg))
    r_later = tile(ktop, r_later, causal)
    lax.fori_loop(0, ktop, lambda i, r: tile(ktop - 1 - i, r, None), r_later)
    for pp in range(pg):
        o_ref[0, :, pp * LANES:(pp + 1) * LANES] = acc_ref[pp]


def _sb_prompt(q, k, v, bias, pg=8):
    b, t, d = q.shape
    blk = 256 if t % 256 == 0 else (128 if t % 128 == 0 else t)
    kt = blk
    assert blk % (2 * SUBLANES) == 0 and kt % blk == 0 and (d // LANES) % pg == 0
    w = pg * LANES
    return pl.pallas_call(
        functools.partial(_sb_prompt_kernel, blk=blk, kt=kt, pg=pg),
        grid=(b, d // w, t // blk),
        in_specs=[
            pl.BlockSpec(memory_space=pltpu.SMEM),
            pl.BlockSpec((1, blk, w), lambda i, p, j: (i, j, p)),
            pl.BlockSpec((1, t, w), lambda i, p, j: (i, 0, p)),
            pl.BlockSpec((1, t, w), lambda i, p, j: (i, 0, p)),
        ],
        out_specs=pl.BlockSpec((1, blk, w), lambda i, p, j: (i, j, p)),
        out_shape=jax.ShapeDtypeStruct((b, t, d), F32),
        scratch_shapes=[pltpu.VMEM((pg, blk, LANES), F32)],
        compiler_params=_params("parallel", "parallel", "arbitrary"), name="sb_prompt",
    )(bias, q, k, v)


def _sb_paged_kernel(pt_ref, bias_ref, q_ref, kn_ref, vn_ref, *refs, scale, pps, tq):
    k_refs = refs[:pps]
    v_refs = refs[pps:2 * pps]
    o_ref, qx_ref, acc_ref, r_ref = refs[2 * pps:]
    s = pl.program_id(1)
    hd = SB_HEAD_DIM
    rows = SB_HEADS * tq
    d = SB_HEADS * hd

    @pl.when(s == 0)
    def _():
        q = q_ref[0] * scale
        qt = jnp.concatenate([q] * SB_HEADS, axis=0)
        rh = _iota2((rows, d), 0) // tq
        lh = _iota2((rows, d), 1) // hd
        qx_ref[...] = jnp.where(rh == lh, qt, 0.0).astype(BF16)
        z = _mm_nt(qx_ref[...], kn_ref[0]) + bias_ref[:, :tq]
        rr = _iota2((rows, tq), 0)
        cc = _iota2((rows, tq), 1)
        mask = cc < (rr & (tq - 1))
        ur = _iota2((tq, tq), 0)
        uc = _iota2((tq, tq), 1)
        att, r_later = _sb_tile(z, jnp.zeros((rows, 1), F32), (ur >= uc).astype(BF16), mask)
        acc_ref[...] = _mm(att, vn_ref[0])
        r_ref[...] = jnp.broadcast_to(r_later, r_ref.shape)

    ur = _iota2((PAGE_SIZE, PAGE_SIZE), 0)
    uc = _iota2((PAGE_SIZE, PAGE_SIZE), 1)
    u_bf = (ur >= uc).astype(BF16)
    qx = qx_ref[...]
    zs = [_mm(qx, k_refs[j][0]) + bias_ref[...] for j in range(pps)]
    atts, (r_later,) = _sb_tiles(zs, [r_ref[...]], u_bf, None, chained=True)
    acc = acc_ref[...]
    for j in range(pps):
        acc = acc + _mm_nt(atts[j], v_refs[j][0])
    acc_ref[...] = acc
    r_ref[...] = r_later

    @pl.when(s == pl.num_programs(1) - 1)
    def _():
        lh = _iota2((tq, d), 1) // hd
        out = jnp.zeros((tq, d), F32)
        for h in range(SB_HEADS):
            out = out + jnp.where(lh == h, acc[h * tq:(h + 1) * tq, :], 0.0)
        o_ref[0] = out


def _sb_paged(q, k_new, v_new, pool_k, pool_v, page_table, pool_base, bias):
    b, tq, d = q.shape
    n_pages = page_table.shape[1]
    pps = PAGES_PER_STEP
    while n_pages % pps:
        pps //= 2
    assert tq & (tq - 1) == 0 and (SB_HEADS * tq) % SUBLANES == 0
    rows = SB_HEADS * tq
    bias_rows = jnp.broadcast_to(jnp.repeat(bias, tq)[:, None], (rows, PAGE_SIZE))

    def page_map(j):
        return lambda i, s, pt: (pool_base + pt[i, n_pages - 1 - (s * pps + j)], 0, 0)

    page_specs = [pl.BlockSpec((1, d, PAGE_SIZE), page_map(j)) for j in range(pps)]
    row_spec = pl.BlockSpec((1, tq, d), lambda i, s, pt: (i, 0, 0))
    grid_spec = pltpu.PrefetchScalarGridSpec(
        num_scalar_prefetch=1,
        grid=(b, n_pages // pps),
        in_specs=[pl.BlockSpec((rows, PAGE_SIZE), lambda i, s, pt: (0, 0)), row_spec, row_spec, row_spec]
        + page_specs + page_specs,
        out_specs=row_spec,
        scratch_shapes=[pltpu.VMEM((rows, d), BF16), pltpu.VMEM((rows, d), F32),
                        pltpu.VMEM((rows, PAGE_SIZE), F32)],
    )
    return pl.pallas_call(
        functools.partial(_sb_paged_kernel, scale=SB_HEAD_DIM ** -0.5, pps=pps, tq=tq),
        grid_spec=grid_spec,
        out_shape=jax.ShapeDtypeStruct((b, tq, d), F32),
        compiler_params=_params("parallel", "arbitrary"), name="sb_paged",
    )(page_table, bias_rows, q, k_new, v_new, *([pool_k] * pps), *([pool_v] * pps))


def _rwkv_state_to_pairs(s):
    b = s.shape[0]
    s = s.reshape(b, RWKV_PAIRS, 2, RWKV_HEAD_DIM, RWKV_HEAD_DIM)
    z = jnp.zeros_like(s[:, :, 0])
    top = jnp.concatenate([s[:, :, 0], z], axis=-1)
    bot = jnp.concatenate([z, s[:, :, 1]], axis=-1)
    return jnp.concatenate([top, bot], axis=-2)


def _rwkv_state_from_pairs(sp):
    b = sp.shape[0]
    hd = RWKV_HEAD_DIM
    return jnp.stack([sp[:, :, :hd, :hd], sp[:, :, hd:, hd:]], axis=2).reshape(b, 2 * RWKV_PAIRS, hd, hd)


def _prep_weights(p):
    w = {}
    ev = p["ev_w_in"][0]
    gdn_in = 4 * GDN_WIDTH
    w["ev_g"] = ev[:, :gdn_in].astype(BF16)
    w["ev_r"] = ev[:, gdn_in + 2 * GDN_HEADS:].astype(BF16)
    w["ev_bg"] = jnp.pad(ev[:, gdn_in:gdn_in + 2 * GDN_HEADS],
                         ((0, 0), (0, LANES - 2 * GDN_HEADS))).astype(BF16)
    w_out = p["ev_w_out"][0].astype(BF16)
    w["ev_out_a"], w["ev_out_b"] = w_out[:GDN_WIDTH], w_out[GDN_WIDTH:]
    hp = jnp.zeros((SUBLANES, LANES), F32)
    hp = hp.at[0, GDN_HEADS:2 * GDN_HEADS].set(p["gdn_a_log"][0])
    hp = hp.at[1, GDN_HEADS:2 * GDN_HEADS].set(p["gdn_dt_bias"][0])
    w["gdn_hp"] = hp
    names = ("rwkv_w0", "rwkv_a0", "rwkv_k_k", "rwkv_k_a", "rwkv_r_k", "rwkv_gn_g", "rwkv_gn_b")
    rows = [p[n][0] for n in names] + [jnp.zeros((RWKV_WIDTH,), F32)]
    w["rwkv_vecs"] = jnp.stack(rows)
    w2, a2 = p["rwkv_w2"][0], p["rwkv_a2"][0]
    zero = jnp.zeros_like(w2)
    w["rwkv_w2a"] = jnp.concatenate(
        [jnp.concatenate([w2, zero], axis=1), jnp.concatenate([zero, a2], axis=1)], axis=0).astype(BF16)
    w["rwkv_g2"] = p["rwkv_g2"][0].astype(BF16)
    sb = p["sb_w_in"][0].astype(BF16)
    d = sb.shape[0]
    w["sb_q"], w["sb_k"], w["sb_v"] = sb[:, :d], sb[:, d:2 * d], sb[:, 2 * d:]
    w["sb_out"] = p["sb_w_out"][0].astype(BF16)
    for n in ("mem_w_q", "mem_w_k", "mem_w_v", "mem_w_o", "ffn_w_in", "ffn_w_out"):
        w[n] = p[n].astype(BF16)
    return w


def _trunk(x, mem_k, mem_v, mem_stride, conv_buf, s_gdn, s_rwkv, shift, sb_past, p, w):
    b, t, d = x.shape
    m = b * t
    gp, rp, bg = _norm_proj(x.reshape(m, d), p["norm_mix"][0], [w["ev_g"], w["ev_r"], w["ev_bg"]])
    o_a, gdn_new, conv_new = _gdn(gp.reshape(b, t, -1), bg.reshape(b, t, LANES), conv_buf, s_gdn,
                                  p["gdn_conv_w"][0], w["gdn_hp"], p["gdn_norm"][0])
    o_b, rwkv_new, shift_new = _rwkv(rp.reshape(b, t, RWKV_IN), shift, _rwkv_state_to_pairs(s_rwkv),
                                     p["rwkv_mu"][0], w["rwkv_vecs"], w["rwkv_w2a"], w["rwkv_g2"])
    x3 = _mix_mem_attn(x, [o_a, o_b], [w["ev_out_a"], w["ev_out_b"]], p["norm_mem"][0],
                       w["mem_w_q"][0], w["mem_w_o"][0], mem_k, mem_v, 0)
    x4 = _ffn(x3.reshape(m, d), p["norm_ffn"][0], w["ffn_w_in"][0], w["ffn_w_out"][0],
              p["norm_final"], False)
    sb_w = [w["sb_q"], w["sb_k"], w["sb_v"]]
    heads = (b, t, SB_HEADS, SB_HEAD_DIM)
    if sb_past is None:
        outs = [(0, BF16, SB_HEAD_DIM ** -0.5, False), (1, BF16, 1.0, False), (2, BF16, 1.0, False),
                (1, F32, 1.0, True), (2, F32, 1.0, True)]
        q_bf, k_bf, v_bf, k_t, v_t = _norm_proj(x4, p["norm_mix"][1], sb_w, outs, seq_len=t)
        y = _sb_prompt(q_bf.reshape(b, t, d), k_bf.reshape(b, t, d), v_bf.reshape(b, t, d), p["sb_bias"][0])
        k_out, v_out = (jnp.transpose(a.reshape(b, SB_HEADS, SB_HEAD_DIM, t), (0, 3, 1, 2)) for a in (k_t, v_t))
    else:
        q, k, v = (a.reshape(b, t, d) for a in _norm_proj(x4, p["norm_mix"][1], sb_w))
        pool_k, pool_v, page_table = sb_past
        y = _sb_paged(q, k, v, pool_k, pool_v, page_table, 0, p["sb_bias"][0])
        k_out, v_out = k.reshape(heads), v.reshape(heads)
    x6 = _mix_mem_attn(x4.reshape(b, t, d), [y], [w["sb_out"]], p["norm_mem"][1],
                       w["mem_w_q"][1], w["mem_w_o"][1], mem_k, mem_v, mem_stride)
    y_out = _ffn(x6.reshape(m, d), p["norm_ffn"][1], w["ffn_w_in"][1], w["ffn_w_out"][1],
                 p["norm_final"], True).reshape(b, t, d)
    return (y_out, conv_new[None], gdn_new[None], _rwkv_state_from_pairs(rwkv_new)[None],
            shift_new.reshape(b, RWKV_IN)[None], k_out[None], v_out[None])


def kernel(x_prompt, x_sample, mem_prompt, state_gdn, state_gdn_conv, state_rwkv, state_rwkv_shift,
           cache_sb_k, cache_sb_v, cache_mem_k, cache_mem_v, page_table,
           norm_mix, norm_mem, norm_memtok, norm_ffn, norm_final, ev_w_in, ev_w_out,
           gdn_conv_w, gdn_a_log, gdn_dt_bias, gdn_norm, rwkv_mu, rwkv_w0, rwkv_w2, rwkv_a0, rwkv_a2, rwkv_g2,
           rwkv_k_k, rwkv_k_a, rwkv_r_k, rwkv_gn_g, rwkv_gn_b, sb_w_in, sb_w_out, sb_bias,
           mem_w_q, mem_w_k, mem_w_v, mem_w_o, ffn_w_in, ffn_w_out):
    p = dict(norm_mix=norm_mix, norm_mem=norm_mem, norm_memtok=norm_memtok, norm_ffn=norm_ffn,
             norm_final=norm_final, ev_w_in=ev_w_in, ev_w_out=ev_w_out, gdn_conv_w=gdn_conv_w,
             gdn_a_log=gdn_a_log, gdn_dt_bias=gdn_dt_bias, gdn_norm=gdn_norm, rwkv_mu=rwkv_mu,
             rwkv_w0=rwkv_w0, rwkv_w2=rwkv_w2, rwkv_a0=rwkv_a0, rwkv_a2=rwkv_a2, rwkv_g2=rwkv_g2,
             rwkv_k_k=rwkv_k_k, rwkv_k_a=rwkv_k_a, rwkv_r_k=rwkv_r_k, rwkv_gn_g=rwkv_gn_g,
             rwkv_gn_b=rwkv_gn_b, sb_w_in=sb_w_in, sb_w_out=sb_w_out, sb_bias=sb_bias,
             mem_w_q=mem_w_q, mem_w_k=mem_w_k, mem_w_v=mem_w_v, mem_w_o=mem_w_o,
             ffn_w_in=ffn_w_in, ffn_w_out=ffn_w_out)
    assert ev_w_in.shape[0] == 1 and sb_w_in.shape[0] == 1 and norm_mix.shape[0] == 2
    w = _prep_weights(p)
    bp, tp, d = x_prompt.shape
    bs = x_sample.shape[0]
    n_mem = mem_prompt.shape[1]

    mem2d = mem_prompt.reshape(bp * n_mem, d)
    mem_outs = [(0, BF16, 1.0, False), (1, BF16, 1.0, False), (0, F32, 1.0, "head_tiles"),
                (1, F32, 1.0, "head_tiles")]
    hd = d // MEM_HEADS
    mk_l, mv_l, mkb_l, mvb_l = [], [], [], []
    for layer in range(2):
        mk_bf, mv_bf, mk_t, mv_t = _norm_proj(mem2d, norm_memtok[layer],
                                              [w["mem_w_k"][layer], w["mem_w_v"][layer]], mem_outs)
        mkb_l.append(mk_bf.reshape(bp, n_mem, d))
        mvb_l.append(mv_bf.reshape(bp, n_mem, d))
        for tiles, dst in ((mk_t, mk_l), (mv_t, mv_l)):
            t5 = tiles.reshape(bp, n_mem, hd // LANES, MEM_HEADS, LANES)
            dst.append(jnp.transpose(t5, (0, 1, 3, 2, 4)).reshape(bp, n_mem, MEM_HEADS, hd))
    mem_k_p = jnp.stack(mk_l)
    mem_v_p = jnp.stack(mv_l)
    zeros = functools.partial(jnp.zeros, dtype=F32)
    out_p = _trunk(
        x_prompt, jnp.concatenate(mkb_l, axis=0), jnp.concatenate(mvb_l, axis=0), bp,
        zeros((bp, CONV_WIDTH - 1, GDN_CONV_CH)), zeros((bp, GDN_HEADS, GDN_HEAD_DIM, GDN_HEAD_DIM)),
        zeros((bp, 2 * RWKV_PAIRS, RWKV_HEAD_DIM, RWKV_HEAD_DIM)), zeros((bp, RWKV_IN)), None, p, w)

    n_pool = cache_sb_k.shape[1]
    pool_k = jnp.transpose(cache_sb_k, (0, 1, 3, 4, 2)).reshape(cache_sb_k.shape[0] * n_pool, d, PAGE_SIZE)
    pool_v = jnp.transpose(cache_sb_v, (0, 1, 3, 4, 2)).reshape(cache_sb_v.shape[0] * n_pool, d, PAGE_SIZE)
    out_s = _trunk(
        x_sample, cache_mem_k.reshape(2 * bs, n_mem, d), cache_mem_v.reshape(2 * bs, n_mem, d), bs,
        state_gdn_conv[0], state_gdn[0], state_rwkv[0], state_rwkv_shift[0],
        (pool_k, pool_v, page_table), p, w)

    y_p, conv_p, gdn_p, rwkv_p, shift_p, sbk_p, sbv_p = out_p
    y_s, conv_s, gdn_s, rwkv_s, shift_s, sbk_s, sbv_s = out_s
    return (y_p, y_s, gdn_p, gdn_s, conv_p, conv_s, rwkv_p, rwkv_s, shift_p, shift_s,
            sbk_p, sbk_s, sbv_p, sbv_s, mem_k_p, mem_v_p)
```

```python
import functools
import math

import jax
import jax.numpy as jnp
from jax import lax
from jax.experimental import pallas as pl
from jax.experimental.pallas import tpu as pltpu

F32 = jnp.float32
BF16 = jnp.bfloat16

EPS = 1e-6
RWKV_GN_EPS = 64e-5
LOG2E = math.log2(math.e)
CONV_WIDTH = 4
GDN_HEADS = 4
GDN_HEAD_DIM = 128
GDN_WIDTH = GDN_HEADS * GDN_HEAD_DIM
GDN_CONV_CH = 3 * GDN_WIDTH
RWKV_HEAD_DIM = 64
RWKV_WIDTH = 512
RWKV_PAIRS = RWKV_WIDTH // 128
RWKV_IN = 1792
SB_HEADS = 16
SB_HEAD_DIM = 64
MEM_HEADS = 4
PAGE_SIZE = 128
LANES = 128
SUBLANES = 8
V7X_VMEM_LIMIT_BYTES = 56 * 1024 * 1024
PAGES_PER_STEP = 16


def _params(*sem):
    return pltpu.CompilerParams(dimension_semantics=sem, vmem_limit_bytes=V7X_VMEM_LIMIT_BYTES)


def _mm(a, b):
    return jnp.dot(a.astype(BF16), b.astype(BF16), preferred_element_type=F32)


def _mm_nt(a, b):
    return lax.dot_general(a.astype(BF16), b.astype(BF16), (((1,), (1,)), ((), ())),
                           preferred_element_type=F32)


def _mm_tn(a, b):
    return lax.dot_general(a.astype(BF16), b.astype(BF16), (((0,), (0,)), ((), ())),
                           preferred_element_type=F32)


def _hi_lo(x):
    hi = x.astype(BF16)
    lo = (x - hi.astype(F32)).astype(BF16)
    return hi, lo


def _mm3(a, b):
    ah, al = _hi_lo(a)
    bh, bl = _hi_lo(b)
    d = functools.partial(jnp.dot, preferred_element_type=F32)
    return d(ah, bh) + d(ah, bl) + d(al, bh)


INVERSE_PASSES = 1


def _mm_inv(a, b):
    return _mm3(a, b) if INVERSE_PASSES == 3 else _mm(a, b)


def _mm_exact_lhs(a_bf, x):
    xh, xl = _hi_lo(x)
    return (jnp.dot(a_bf, xh, preferred_element_type=F32)
            + jnp.dot(a_bf, xl, preferred_element_type=F32))


def _rms(x, g):
    return x * lax.rsqrt(jnp.mean(x * x, axis=-1, keepdims=True) + EPS) * g


def _softplus(x):
    return jnp.maximum(x, 0.0) + jnp.log1p(jnp.exp(-jnp.abs(x)))


def _iota2(shape, dim):
    return lax.broadcasted_iota(jnp.int32, shape, dim)


def _row_tile(m, want):
    t = min(m, want)
    while m % t:
        t //= 2
    return t


def _levels(c):
    return max(1, int(math.ceil(math.log2(c))))


def _norm_proj_kernel(x_ref, g_ref, *refs, n_w, outs):
    xn = _rms(x_ref[...], g_ref[...]).astype(BF16)
    prods = [jnp.dot(xn, w_ref[...], preferred_element_type=F32) for w_ref in refs[:n_w]]
    for (wi, dtype, scale, transposed), o_ref in zip(outs, refs[n_w:]):
        y = (prods[wi] if scale == 1.0 else prods[wi] * scale).astype(dtype)
        if transposed == "head_tiles":
            rows, n = y.shape
            hd = n // MEM_HEADS
            for c in range(SUBLANES):
                h, part = c % MEM_HEADS, c // MEM_HEADS
                lo = h * hd + part * LANES
                o_ref[pl.ds(c, rows, stride=SUBLANES), :] = y[:, lo:lo + LANES]
        elif transposed:
            o_ref[0] = y.T
        else:
            o_ref[...] = y


def _norm_proj(x2d, g, ws, outs=None, tm=512, seq_len=None):
    m, d = x2d.shape
    tm = _row_tile(seq_len if seq_len else m, tm)
    if outs is None:
        outs = [(i, F32, 1.0, False) for i in range(len(ws))]
    in_specs = [pl.BlockSpec((tm, d), lambda i: (i, 0)), pl.BlockSpec((1, d), lambda i: (0, 0))]
    in_specs += [pl.BlockSpec(w.shape, lambda i: (0, 0)) for w in ws]
    out_specs, out_shape = [], []
    for wi, dt, _, transposed in outs:
        n = ws[wi].shape[1]
        if transposed == "head_tiles":
            assert n == MEM_HEADS * 2 * LANES
            out_specs.append(pl.BlockSpec((tm * SUBLANES, LANES), lambda i: (i, 0)))
            out_shape.append(jax.ShapeDtypeStruct((m * SUBLANES, LANES), dt))
        elif transposed:
            nt = seq_len // tm
            out_specs.append(pl.BlockSpec((1, n, tm), lambda i, nt=nt: (i // nt, 0, i % nt)))
            out_shape.append(jax.ShapeDtypeStruct((m // seq_len, n, seq_len), dt))
        else:
            out_specs.append(pl.BlockSpec((tm, n), lambda i: (i, 0)))
            out_shape.append(jax.ShapeDtypeStruct((m, n), dt))
    return pl.pallas_call(
        functools.partial(_norm_proj_kernel, n_w=len(ws), outs=tuple(outs)),
        grid=(m // tm,), in_specs=in_specs, out_specs=out_specs, out_shape=out_shape,
        compiler_params=_params("parallel"), name="norm_proj",
    )(x2d, g.reshape(1, d), *ws)


def _ffn_kernel(x_ref, g_ref, wg_ref, wu_ref, wo_ref, gf_ref, o_ref, xn_ref, acc_ref, *, final_norm):
    j = pl.program_id(1)

    @pl.when(j == 0)
    def _():
        xn_ref[...] = _rms(x_ref[...], g_ref[...]).astype(BF16)
        acc_ref[...] = jnp.zeros_like(acc_ref)

    xn = xn_ref[...]
    gt = jnp.dot(xn, wg_ref[...], preferred_element_type=F32)
    ut = jnp.dot(xn, wu_ref[...], preferred_element_type=F32)
    act = (gt * jax.nn.sigmoid(gt) * ut).astype(BF16)
    acc_ref[...] += jnp.dot(act, wo_ref[...], preferred_element_type=F32)

    @pl.when(j == pl.num_programs(1) - 1)
    def _():
        y = x_ref[...] + acc_ref[...]
        if final_norm:
            y = _rms(y, gf_ref[...])
        o_ref[...] = y


def _ffn(x2d, g, w_in, w_out, g_final, final_norm, tm=512, n_chunks=2):
    m, d = x2d.shape
    dff = w_out.shape[0]
    tf = dff // n_chunks
    tm = _row_tile(m, tm)
    return pl.pallas_call(
        functools.partial(_ffn_kernel, final_norm=final_norm),
        grid=(m // tm, n_chunks),
        in_specs=[
            pl.BlockSpec((tm, d), lambda i, j: (i, 0)),
            pl.BlockSpec((1, d), lambda i, j: (0, 0)),
            pl.BlockSpec((d, tf), lambda i, j: (0, j)),
            pl.BlockSpec((d, tf), lambda i, j: (0, n_chunks + j)),
            pl.BlockSpec((tf, d), lambda i, j: (j, 0)),
            pl.BlockSpec((1, d), lambda i, j: (0, 0)),
        ],
        out_specs=pl.BlockSpec((tm, d), lambda i, j: (i, 0)),
        out_shape=jax.ShapeDtypeStruct((m, d), F32),
        scratch_shapes=[pltpu.VMEM((tm, d), BF16), pltpu.VMEM((tm, d), F32)],
        compiler_params=_params("parallel", "arbitrary"), name="ffn",
    )(x2d, g.reshape(1, d), w_in, w_in, w_out, g_final.reshape(1, d))


def _mix_mem_attn_kernel(x_ref, *refs, scale, heads, n_in):
    act_refs, w_refs = refs[:n_in], refs[n_in:2 * n_in]
    g_ref, wq_ref, wo_ref, mk_ref, mv_ref, o_ref = refs[2 * n_in:]
    ns, tm, d = x_ref.shape
    x = x_ref[...].reshape(ns * tm, d)
    for a_ref, w_ref in zip(act_refs, w_refs):
        a = a_ref[...].reshape(ns * tm, a_ref.shape[2])
        x = x + jnp.dot(a.astype(BF16), w_ref[...], preferred_element_type=F32)
    xn = _rms(x, g_ref[...]).astype(BF16)
    q = (jnp.dot(xn, wq_ref[...], preferred_element_type=F32) * scale).astype(BF16)
    hd = d // heads
    parts = hd // LANES
    n_mem = mk_ref.shape[1] // (heads * parts)
    units = [(s, h) for s in range(ns) for h in range(heads)]

    def chunk(ref, s, h, part):
        return ref[s, pl.ds(part * heads + h, n_mem, stride=heads * parts), :].astype(BF16)

    def q_chunk(s, h, part):
        lo = h * hd + part * LANES
        return q[s * tm:(s + 1) * tm, lo:lo + LANES]

    scores = [sum(_mm_nt(q_chunk(s, h, part), chunk(mk_ref, s, h, part)) for part in range(parts))
              for s, h in units]
    es = [jnp.exp(sc - jnp.max(sc, axis=-1, keepdims=True)) for sc in scores]
    pvs = [jnp.concatenate([_mm(e, chunk(mv_ref, s, h, part)) for part in range(parts)], axis=-1)
           / jnp.sum(e, axis=-1, keepdims=True) for e, (s, h) in zip(es, units)]
    o = jnp.concatenate([jnp.concatenate(pvs[s * heads:(s + 1) * heads], axis=-1) for s in range(ns)], axis=0)
    y = x + jnp.dot(o.astype(BF16), wo_ref[...], preferred_element_type=F32)
    o_ref[...] = y.reshape(ns, tm, d)


def _mix_mem_attn(x, acts, ws, g, wq, wo, mk, mv, mem_base, tm=512):
    b, t, d = x.shape
    tm = _row_tile(t, tm)
    ns = 1
    if tm == t:
        while ns < 4 and b % (2 * ns) == 0 and mem_base % (2 * ns) == 0 and 2 * ns * tm <= 512:
            ns *= 2
    assert mk.shape[2] == LANES and (d // MEM_HEADS) % LANES == 0
    seq = lambda width: pl.BlockSpec((ns, tm, width), lambda i, j: (i, j, 0))
    const = lambda shape: pl.BlockSpec(shape, lambda i, j: (0,) * len(shape))
    mem = pl.BlockSpec((ns, mk.shape[1], LANES), lambda i, j: (mem_base // ns + i, 0, 0))
    return pl.pallas_call(
        functools.partial(_mix_mem_attn_kernel, scale=(d // MEM_HEADS) ** -0.5, heads=MEM_HEADS,
                          n_in=len(acts)),
        grid=(b // ns, t // tm),
        in_specs=[seq(d)] + [seq(a.shape[2]) for a in acts] + [const(w_.shape) for w_ in ws]
        + [const((1, d)), const((d, d)), const((d, d)), mem, mem],
        out_specs=seq(d),
        out_shape=jax.ShapeDtypeStruct((b, t, d), F32),
        compiler_params=_params("parallel", "parallel"), name="mix_mem_attn",
    )(x, *acts, *ws, g.reshape(1, d), wq, wo, mk, mv)


def _gdn_kernel(gp_ref, bg_ref, cb_ref, s0_ref, cw_ref, hp_ref, gn_ref, o_ref, so_ref, cv_ref,
                ext_ref, s_ref, *, chunk, levels, nb):
    c = pl.program_id(1)
    C = chunk
    dk = GDN_HEAD_DIM
    tail = CONV_WIDTH - 1

    @pl.when(c == 0)
    def _():
        ext_ref[:, SUBLANES - tail:SUBLANES, :] = cb_ref[...]
        s_ref[...] = s0_ref[...]

    C2 = 2 * C
    r2 = _iota2((C2, C2), 0)
    c2 = _iota2((C2, C2), 1)
    same = (r2 >= C) == (c2 >= C)
    lower = same & (r2 >= c2)
    strict = same & (r2 > c2)
    tri = lower.astype(BF16)
    block_ones = same.astype(BF16)
    eye = (r2 == c2).astype(F32)
    first = _iota2((C2, dk), 0) < C
    lane = _iota2((C, LANES), 1)
    cw = cw_ref[...]

    groups = []
    for bi in range(nb):
        gp = gp_ref[bi]
        u_in = gp[:, :GDN_CONV_CH]
        ext_ref[bi, SUBLANES:SUBLANES + C, :] = u_in
        conv = u_in * cw[tail:tail + 1]
        for j in range(tail):
            lo = SUBLANES - tail + j
            conv = conv + ext_ref[bi, lo:lo + C, :] * cw[j:j + 1]
        new_tail = ext_ref[bi, SUBLANES + C - tail:SUBLANES + C, :]
        ext_ref[bi, SUBLANES - tail:SUBLANES, :] = new_tail
        cv_ref[bi] = new_tail
        qkv = conv * jax.nn.sigmoid(conv)

        bg = bg_ref[bi]
        beta_all = jax.nn.sigmoid(bg)
        g_all = -jnp.exp(hp_ref[0:1, :]) * _softplus(bg + hp_ref[1:2, :])

        def head(h, qkv=qkv, beta_all=beta_all, g_all=g_all):
            qh = qkv[:, h * dk:(h + 1) * dk]
            kh = qkv[:, GDN_WIDTH + h * dk:GDN_WIDTH + (h + 1) * dk]
            vh = qkv[:, 2 * GDN_WIDTH + h * dk:2 * GDN_WIDTH + (h + 1) * dk]
            qn = qh * lax.rsqrt(jnp.sum(qh * qh, axis=-1, keepdims=True) + EPS) * (dk ** -0.5)
            kn = kh * lax.rsqrt(jnp.sum(kh * kh, axis=-1, keepdims=True) + EPS)
            beta = jnp.sum(jnp.where(lane == h, beta_all, 0.0), axis=-1, keepdims=True)
            g = jnp.sum(jnp.where(lane == GDN_HEADS + h, g_all, 0.0), axis=-1, keepdims=True)
            return qn, kn, vh, beta, g

        for gi in range(GDN_HEADS // 2):
            h0, h1 = head(2 * gi), head(2 * gi + 1)
            qs, ks, vs, beta, g = (jnp.concatenate([a, b], axis=0) for a, b in zip(h0, h1))
            gcum = _mm_exact_lhs(tri, jnp.broadcast_to(g, (C2, dk)))
            g_cc = jnp.broadcast_to(g, (C2, C2))
            gcum_i = _mm_exact_lhs(tri, g_cc)
            gcum_j = _mm_exact_lhs(block_ones, jnp.where(same & (r2 <= c2), g_cc, 0.0))
            decay = jnp.where(lower, jnp.exp(jnp.minimum(gcum_i - gcum_j, 0.0)), 0.0)
            gam = jnp.exp(gcum)
            glast = jnp.where(first, gcum[C - 1:C, :], gcum[C2 - 1:C2, :])
            x = jnp.where(strict, -(beta * _mm_nt(ks, ks) * decay), 0.0)
            groups.append(dict(bi=bi, gi=gi, gp=gp, qs=qs, ks=ks, vs=vs, beta=beta, gam=gam, glast=glast,
                               gcum=gcum, x=x, qkd=_mm_nt(qs, ks) * decay))

    invs = [eye + gr["x"] for gr in groups]
    pws = [gr["x"] for gr in groups]
    for _ in range(levels - 1):
        pws = [_mm_inv(pw, pw) for pw in pws]
        invs = [inv + _mm_inv(inv, pw) for inv, pw in zip(invs, pws)]

    heads = [(n, j, slice(j * C, (j + 1) * C)) for n in range(len(groups)) for j in range(2)]
    w_mats = [_mm(inv, gr["beta"] * gr["gam"] * gr["ks"]) for gr, inv in zip(groups, invs)]
    u0s = [_mm(inv, gr["beta"] * gr["vs"]) for gr, inv in zip(groups, invs)]
    kdecs = [gr["ks"] * jnp.exp(gr["glast"] - gr["gcum"]) for gr in groups]
    states = [s_ref[groups[n]["bi"], 2 * groups[n]["gi"] + j] for n, j, _ in heads]
    ws_s = [_mm(w_mats[n][rows], s) for (n, _, rows), s in zip(heads, states)]
    qs_s = [_mm(groups[n]["qs"][rows], s) for (n, _, rows), s in zip(heads, states)]
    us = [u0s[n][rows] - ws for (n, _, rows), ws in zip(heads, ws_s)]
    u_g = [jnp.concatenate(us[2 * n:2 * n + 2], axis=0) for n in range(len(groups))]
    o_g = [gr["gam"] * jnp.concatenate(qs_s[2 * n:2 * n + 2], axis=0) + _mm(gr["qkd"], u_g[n])
           for n, gr in enumerate(groups)]
    upds = [_mm_tn(kdecs[n][rows], u) for (n, _, rows), u in zip(heads, us)]
    for (n, j, rows), s, upd in zip(heads, states, upds):
        gr = groups[n]
        bi, h = gr["bi"], 2 * gr["gi"] + j
        s_ref[bi, h] = jnp.exp(gr["glast"][j * C:j * C + 1, :]) * s + upd
        zh = gr["gp"][:, GDN_CONV_CH + h * dk:GDN_CONV_CH + (h + 1) * dk]
        o_ref[bi, :, h * dk:(h + 1) * dk] = _rms(o_g[n][rows], gn_ref[...]) * (zh * jax.nn.sigmoid(zh))

    @pl.when(c == pl.num_programs(1) - 1)
    def _():
        so_ref[...] = s_ref[...]


def _seqs_per_step(b):
    for nb in (8, 4, 2):
        if b % nb == 0:
            return nb
    return 1


def _gdn(gp, bg, conv_buf, s0, conv_w, head_params, gnorm):
    b, t, _ = gp.shape
    chunk = min(64, t)
    nb = _seqs_per_step(b)
    assert t % chunk == 0 and chunk % SUBLANES == 0 and chunk >= CONV_WIDTH - 1
    state = (nb, GDN_HEADS, GDN_HEAD_DIM, GDN_HEAD_DIM)
    return pl.pallas_call(
        functools.partial(_gdn_kernel, chunk=chunk, levels=_levels(chunk), nb=nb),
        grid=(b // nb, t // chunk),
        in_specs=[
            pl.BlockSpec((nb, chunk, gp.shape[2]), lambda i, j: (i, j, 0)),
            pl.BlockSpec((nb, chunk, LANES), lambda i, j: (i, j, 0)),
            pl.BlockSpec((nb, CONV_WIDTH - 1, GDN_CONV_CH), lambda i, j: (i, 0, 0)),
            pl.BlockSpec(state, lambda i, j: (i, 0, 0, 0)),
            pl.BlockSpec((CONV_WIDTH, GDN_CONV_CH), lambda i, j: (0, 0)),
            pl.BlockSpec((SUBLANES, LANES), lambda i, j: (0, 0)),
            pl.BlockSpec((1, GDN_HEAD_DIM), lambda i, j: (0, 0)),
        ],
        out_specs=[
            pl.BlockSpec((nb, chunk, GDN_WIDTH), lambda i, j: (i, j, 0)),
            pl.BlockSpec(state, lambda i, j: (i, 0, 0, 0)),
            pl.BlockSpec((nb, CONV_WIDTH - 1, GDN_CONV_CH), lambda i, j: (i, 0, 0)),
        ],
        out_shape=[
            jax.ShapeDtypeStruct((b, t, GDN_WIDTH), F32),
            jax.ShapeDtypeStruct((b, GDN_HEADS, GDN_HEAD_DIM, GDN_HEAD_DIM), F32),
            jax.ShapeDtypeStruct((b, CONV_WIDTH - 1, GDN_CONV_CH), F32),
        ],
        scratch_shapes=[
            pltpu.VMEM((nb, chunk + SUBLANES, GDN_CONV_CH), F32),
            pltpu.VMEM(state, F32),
        ],
        compiler_params=_params("parallel", "arbitrary"), name="gdn",
    )(gp, bg, conv_buf, s0, conv_w, head_params, gnorm.reshape(1, GDN_HEAD_DIM))


def _rwkv_kernel(rp_ref, sh_ref, s0_ref, mu_ref, vec_ref, w2a_ref, g2_ref, o_ref, so_ref, sho_ref,
                 last_ref, s_ref, *, chunk, levels, nb):
    c = pl.program_id(1)
    C = chunk
    W = RWKV_WIDTH
    hd = RWKV_HEAD_DIM

    @pl.when(c == 0)
    def _():
        last_ref[...] = sh_ref[...]
        s_ref[...] = s0_ref[...]

    vec = vec_ref[...]
    lane = _iota2((C, LANES), 1)
    m0 = lane < hd
    rowi = _iota2((C, RWKV_IN), 0)

    r128 = _iota2((LANES, LANES), 0)
    c128 = _iota2((LANES, LANES), 1)
    bd_mask = (r128 >= hd) == (c128 >= hd)
    bd = bd_mask.astype(BF16)
    rc = _iota2((C, C), 0)
    cc = _iota2((C, C), 1)
    tri = (rc >= cc).astype(BF16)
    r2 = _iota2((2 * C, 2 * C), 0)
    c2 = _iota2((2 * C, 2 * C), 1)
    same = (r2 >= C) == (c2 >= C)
    strict = same & (r2 > c2)
    incl = same & (r2 >= c2)
    eye2 = (r2 == c2).astype(F32)

    def stack(x):
        return jnp.concatenate([x, x], axis=0)

    def stack_masked(x):
        return jnp.concatenate([jnp.where(m0, x, 0.0), jnp.where(m0, 0.0, x)], axis=0)

    def sel(z):
        return jnp.where(m0, z[:C], z[C:])

    pairs = []
    for bi in range(nb):
        rp = rp_ref[bi]
        prev = jnp.where(rowi == 0, last_ref[bi], pltpu.roll(rp, 1, 0))
        xr = rp + (prev - rp) * mu_ref[...]
        last_row = rp[C - 1:C, :]
        last_ref[bi] = last_row
        sho_ref[bi] = last_row

        r_all = xr[:, :W]
        kr = xr[:, W:2 * W]
        v_all = xr[:, 2 * W:3 * W]
        pwa = xr[:, 3 * W:3 * W + LANES]
        pg = xr[:, 3 * W + LANES:]
        wa = _mm(jnp.where(m0, jnp.tanh(pwa), pwa), w2a_ref[...])
        w_raw = vec[0:1] + wa[:, :W]
        a_all = jax.nn.sigmoid(vec[1:2] + wa[:, W:])
        lw_all = -jnp.exp(-_softplus(-w_raw) - 0.5)
        gate = _mm(jax.nn.sigmoid(pg), g2_ref[...])
        kkp = kr * vec[2:3]
        k2_all = kr * (1.0 + (a_all - 1.0) * vec[3:4])

        for p in range(RWKV_PAIRS):
            sl = slice(p * LANES, (p + 1) * LANES)
            r = r_all[:, sl]
            k2 = k2_all[:, sl]
            v = v_all[:, sl]
            lw = lw_all[:, sl]
            kk0 = kkp[:, sl]
            kk = kk0 * lax.rsqrt(_mm(kk0 * kk0, bd) + EPS)

            gcum = _mm_exact_lhs(tri, lw)
            e_in = jnp.exp(gcum)
            e_inv = jnp.exp(-gcum)
            abar = -kk * jnp.exp(gcum - lw)
            bbar = kk * a_all[:, sl] * e_inv
            kbar = k2 * e_inv
            rbar = r * e_in

            a_s = stack_masked(abar)
            r_s = stack_masked(rbar)
            b_c = stack(bbar)
            k_c = stack(kbar)
            pairs.append(dict(
                bi=bi, p=p, sl=sl, r=r, k2=k2, v=v, abar=abar, bbar=bbar, kbar=kbar, rbar=rbar,
                e_last=e_in[C - 1:C, :], gate=gate[:, sl],
                l_ab=jnp.where(strict, _mm_nt(a_s, b_c), 0.0), l_ak=jnp.where(strict, _mm_nt(a_s, k_c), 0.0),
                m_rb=jnp.where(incl, _mm_nt(r_s, b_c), 0.0), m_rk=jnp.where(incl, _mm_nt(r_s, k_c), 0.0)))

    invs = [eye2 + pr["l_ab"] for pr in pairs]
    pws = [pr["l_ab"] for pr in pairs]
    for _ in range(levels - 1):
        pws = [_mm_inv(pw, pw) for pw in pws]
        invs = [inv + _mm_inv(inv, pw) for inv, pw in zip(invs, pws)]

    inv_n = 1.0 / hd
    v_cs = [stack(pr["v"]) for pr in pairs]
    lakv = [_mm(pr["l_ak"], v_c) for pr, v_c in zip(pairs, v_cs)]
    mrkv = [_mm(pr["m_rk"], v_c) for pr, v_c in zip(pairs, v_cs)]
    bonus = [_mm(pr["r"] * pr["k2"] * vec[4:5, pr["sl"]], bd) * pr["v"] for pr in pairs]
    states = [s_ref[pr["bi"], pr["p"]] for pr in pairs]
    ars = [_mm_nt(jnp.concatenate([pr["abar"], pr["rbar"]], axis=0), s) for pr, s in zip(pairs, states)]
    rhss = [ar[:C] + sel(x) for ar, x in zip(ars, lakv)]
    us = [sel(_mm(inv, stack(rhs))) for inv, rhs in zip(invs, rhss)]
    ys = [ar[C:] + sel(_mm(pr["m_rb"], stack(u)) + x) for ar, pr, u, x in zip(ars, pairs, us, mrkv)]
    upds = [_mm_tn(jnp.concatenate([u, pr["v"]], axis=0), jnp.concatenate([pr["bbar"], pr["kbar"]], axis=0))
            for u, pr in zip(us, pairs)]
    for pr, s, upd in zip(pairs, states, upds):
        s_ref[pr["bi"], pr["p"]] = (s + jnp.where(bd_mask, upd, 0.0)) * pr["e_last"]
    ycs = [y - _mm(y, bd) * inv_n for y in ys]
    vrs = [_mm(yc * yc, bd) * inv_n for yc in ycs]
    for pr, yc, var, bon in zip(pairs, ycs, vrs, bonus):
        sl = pr["sl"]
        yn = yc * lax.rsqrt(var + RWKV_GN_EPS) * vec[5:6, sl] + vec[6:7, sl]
        o_ref[pr["bi"], :, sl] = (yn + bon) * pr["gate"]

    @pl.when(c == pl.num_programs(1) - 1)
    def _():
        so_ref[...] = s_ref[...]


def _rwkv(rp, shift, s0_pairs, mu, vecs, w2a, g2):
    b, t, _ = rp.shape
    chunk = min(64, t)
    nb = _seqs_per_step(b)
    assert t % chunk == 0 and chunk % SUBLANES == 0
    state = (nb, RWKV_PAIRS, LANES, LANES)
    return pl.pallas_call(
        functools.partial(_rwkv_kernel, chunk=chunk, levels=_levels(chunk), nb=nb),
        grid=(b // nb, t // chunk),
        in_specs=[
            pl.BlockSpec((nb, chunk, RWKV_IN), lambda i, j: (i, j, 0)),
            pl.BlockSpec((nb, 1, RWKV_IN), lambda i, j: (i, 0, 0)),
            pl.BlockSpec(state, lambda i, j: (i, 0, 0, 0)),
            pl.BlockSpec((1, RWKV_IN), lambda i, j: (0, 0)),
            pl.BlockSpec((SUBLANES, RWKV_WIDTH), lambda i, j: (0, 0)),
            pl.BlockSpec(w2a.shape, lambda i, j: (0, 0)),
            pl.BlockSpec(g2.shape, lambda i, j: (0, 0)),
        ],
        out_specs=[
            pl.BlockSpec((nb, chunk, RWKV_WIDTH), lambda i, j: (i, j, 0)),
            pl.BlockSpec(state, lambda i, j: (i, 0, 0, 0)),
            pl.BlockSpec((nb, 1, RWKV_IN), lambda i, j: (i, 0, 0)),
        ],
        out_shape=[
            jax.ShapeDtypeStruct((b, t, RWKV_WIDTH), F32),
            jax.ShapeDtypeStruct((b, RWKV_PAIRS, LANES, LANES), F32),
            jax.ShapeDtypeStruct((b, 1, RWKV_IN), F32),
        ],
        scratch_shapes=[
            pltpu.VMEM((nb, 1, RWKV_IN), F32),
            pltpu.VMEM(state, F32),
        ],
        compiler_params=_params("parallel", "arbitrary"), name="rwkv7",
    )(rp, shift.reshape(b, 1, RWKV_IN), s0_pairs, mu.reshape(1, RWKV_IN), vecs, w2a, g2)


def _sb_tile(z, r_later, u_bf, mask):
    (att,), (r_new,) = _sb_tiles([z], [r_later], u_bf, mask, chained=False)
    return att, r_new


def _sb_tiles(zs, r_in, u_bf, mask, chained):
    sps = [jnp.maximum(z, 0.0) + jnp.log(1.0 + jnp.exp2(jnp.abs(z) * (-LOG2E))) for z in zs]
    if mask is not None:
        sps = [jnp.where(mask, sp, 0.0) for sp in sps]
    cums = [jnp.dot(sp.astype(BF16), u_bf, preferred_element_type=F32) for sp in sps]
    sums = [jnp.sum(sp, axis=-1, keepdims=True) for sp in sps]
    if chained:
        rs = [r_in[0]]
        for s in sums:
            rs.append(rs[-1] + s)
        r_tiles, r_out = rs[:-1], rs[-1:]
    else:
        r_tiles = r_in
        r_out = [r + s for r, s in zip(r_in, sums)]
    atts = [jnp.exp(z - cum - r) for z, cum, r in zip(zs, cums, r_tiles)]
    if mask is not None:
        atts = [jnp.where(mask, att, 0.0) for att in atts]
    return atts, r_out


def _sb_prompt_kernel(bias_ref, q_ref, k_ref, v_ref, o_ref, acc_ref, *, blk, kt, pg):
    g = pl.program_id(1)
    qi = pl.program_id(2)
    hd = SB_HEAD_DIM
    m0 = _iota2((blk, LANES), 1) < hd
    u_bf = (_iota2((kt, kt), 0) >= _iota2((kt, kt), 1)).astype(BF16)
    ktop = (qi * blk) // kt
    causal = (ktop * kt + _iota2((blk, kt), 1)) < (qi * blk + _iota2((blk, kt), 0))
    qs, bias = [], []
    for pp in range(pg):
        qp = q_ref[0, :, pp * LANES:(pp + 1) * LANES]
        zero = jnp.zeros_like(qp)
        qs.append((jnp.where(m0, qp, zero), jnp.where(m0, zero, qp)))
        head = 2 * (g * pg + pp)
        bias.append((bias_ref[head], bias_ref[head + 1]))
    acc_ref[...] = jnp.zeros_like(acc_ref)

    def tile(kj, r_later, mask):
        start = pl.multiple_of(kj * kt, kt)
        kbs = [k_ref[0, pl.ds(start, kt), pp * LANES:(pp + 1) * LANES] for pp in range(pg)]
        vbs = [v_ref[0, pl.ds(start, kt), pp * LANES:(pp + 1) * LANES] for pp in range(pg)]
        zs = [lax.dot_general(qs[pp][h], kbs[pp], (((1,), (1,)), ((), ())),
                              preferred_element_type=F32) + bias[pp][h]
              for pp in range(pg) for h in range(2)]
        atts, r_new = _sb_tiles(zs, list(r_later), u_bf, mask, chained=False)
        pvs = [jnp.dot(att.astype(BF16), vbs[i // 2], preferred_element_type=F32)
               for i, att in enumerate(atts)]
        for pp in range(pg):
            acc_ref[pp] += jnp.where(m0, pvs[2 * pp], pvs[2 * pp + 1])
        return tuple(r_new)

    r_later = tuple(jnp.zeros((blk, 1), F32) for _ in range(2 * pg))
    r_later = tile(ktop, r_later, causal)
    lax.fori_loop(0, ktop, lambda i, r: tile(ktop - 1 - i, r, None), r_later)
    for pp in range(pg):
        o_ref[0, :, pp * LANES:(pp + 1) * LANES] = acc_ref[pp]


def _sb_prompt(q, k, v, bias, pg=8):
    b, t, d = q.shape
    blk = 256 if t % 256 == 0 else (128 if t % 128 == 0 else t)
    kt = blk
    assert blk % (2 * SUBLANES) == 0 and kt % blk == 0 and (d // LANES) % pg == 0
    w = pg * LANES
    return pl.pallas_call(
        functools.partial(_sb_prompt_kernel, blk=blk, kt=kt, pg=pg),
        grid=(b, d // w, t // blk),
        in_specs=[
            pl.BlockSpec(memory_space=pltpu.SMEM),
            pl.BlockSpec((1, blk, w), lambda i, p, j: (i, j, p)),
            pl.BlockSpec((1, t, w), lambda i, p, j: (i, 0, p)),
            pl.BlockSpec((1, t, w), lambda i, p, j: (i, 0, p)),
        ],
        out_specs=pl.BlockSpec((1, blk, w), lambda i, p, j: (i, j, p)),
        out_shape=jax.ShapeDtypeStruct((b, t, d), F32),
        scratch_shapes=[pltpu.VMEM((pg, blk, LANES), F32)],
        compiler_params=_params("parallel", "parallel", "arbitrary"), name="sb_prompt",
    )(bias, q, k, v)


def _sb_paged_kernel(pt_ref, bias_ref, q_ref, kn_ref, vn_ref, *refs, scale, pps, tq):
    k_refs = refs[:pps]
    v_refs = refs[pps:2 * pps]
    o_ref, qx_ref, acc_ref, r_ref = refs[2 * pps:]
    s = pl.program_id(1)
    hd = SB_HEAD_DIM
    rows = SB_HEADS * tq
    d = SB_HEADS * hd

    @pl.when(s == 0)
    def _():
        q = q_ref[0] * scale
        qt = jnp.concatenate([q] * SB_HEADS, axis=0)
        rh = _iota2((rows, d), 0) // tq
        lh = _iota2((rows, d), 1) // hd
        qx_ref[...] = jnp.where(rh == lh, qt, 0.0).astype(BF16)
        z = _mm_nt(qx_ref[...], kn_ref[0]) + bias_ref[:, :tq]
        rr = _iota2((rows, tq), 0)
        cc = _iota2((rows, tq), 1)
        mask = cc < (rr & (tq - 1))
        ur = _iota2((tq, tq), 0)
        uc = _iota2((tq, tq), 1)
        att, r_later = _sb_tile(z, jnp.zeros((rows, 1), F32), (ur >= uc).astype(BF16), mask)
        acc_ref[...] = _mm(att, vn_ref[0])
        r_ref[...] = jnp.broadcast_to(r_later, r_ref.shape)

    ur = _iota2((PAGE_SIZE, PAGE_SIZE), 0)
    uc = _iota2((PAGE_SIZE, PAGE_SIZE), 1)
    u_bf = (ur >= uc).astype(BF16)
    qx = qx_ref[...]
    zs = [_mm(qx, k_refs[j][0]) + bias_ref[...] for j in range(pps)]
    atts, (r_later,) = _sb_tiles(zs, [r_ref[...]], u_bf, None, chained=True)
    acc = acc_ref[...]
    for j in range(pps):
        acc = acc + _mm_nt(atts[j], v_refs[j][0])
    acc_ref[...] = acc
    r_ref[...] = r_later

    @pl.when(s == pl.num_programs(1) - 1)
    def _():
        lh = _iota2((tq, d), 1) // hd
        out = jnp.zeros((tq, d), F32)
        for h in range(SB_HEADS):
            out = out + jnp.where(lh == h, acc[h * tq:(h + 1) * tq, :], 0.0)
        o_ref[0] = out


def _sb_paged(q, k_new, v_new, pool_k, pool_v, page_table, pool_base, bias):
    b, tq, d = q.shape
    n_pages = page_table.shape[1]
    pps = PAGES_PER_STEP
    while n_pages % pps:
        pps //= 2
    assert tq & (tq - 1) == 0 and (SB_HEADS * tq) % SUBLANES == 0
    rows = SB_HEADS * tq
    bias_rows = jnp.broadcast_to(jnp.repeat(bias, tq)[:, None], (rows, PAGE_SIZE))

    def page_map(j):
        return lambda i, s, pt: (pool_base + pt[i, n_pages - 1 - (s * pps + j)], 0, 0)

    page_specs = [pl.BlockSpec((1, d, PAGE_SIZE), page_map(j)) for j in range(pps)]
    row_spec = pl.BlockSpec((1, tq, d), lambda i, s, pt: (i, 0, 0))
    grid_spec = pltpu.PrefetchScalarGridSpec(
        num_scalar_prefetch=1,
        grid=(b, n_pages // pps),
        in_specs=[pl.BlockSpec((rows, PAGE_SIZE), lambda i, s, pt: (0, 0)), row_spec, row_spec, row_spec]
        + page_specs + page_specs,
        out_specs=row_spec,
        scratch_shapes=[pltpu.VMEM((rows, d), BF16), pltpu.VMEM((rows, d), F32),
                        pltpu.VMEM((rows, PAGE_SIZE), F32)],
    )
    return pl.pallas_call(
        functools.partial(_sb_paged_kernel, scale=SB_HEAD_DIM ** -0.5, pps=pps, tq=tq),
        grid_spec=grid_spec,
        out_shape=jax.ShapeDtypeStruct((b, tq, d), F32),
        compiler_params=_params("parallel", "arbitrary"), name="sb_paged",
    )(page_table, bias_rows, q, k_new, v_new, *([pool_k] * pps), *([pool_v] * pps))


def _mem_head_tiles(a):
    n_l, b, n_mem, h, hd = a.shape
    parts = hd // LANES
    a = a.reshape(n_l, b, n_mem, h, parts, LANES)
    return jnp.transpose(a, (0, 1, 2, 4, 3, 5)).reshape(n_l * b, n_mem * parts * h, LANES)


def _rwkv_state_to_pairs(s):
    b = s.shape[0]
    s = s.reshape(b, RWKV_PAIRS, 2, RWKV_HEAD_DIM, RWKV_HEAD_DIM)
    z = jnp.zeros_like(s[:, :, 0])
    top = jnp.concatenate([s[:, :, 0], z], axis=-1)
    bot = jnp.concatenate([z, s[:, :, 1]], axis=-1)
    return jnp.concatenate([top, bot], axis=-2)


def _rwkv_state_from_pairs(sp):
    b = sp.shape[0]
    hd = RWKV_HEAD_DIM
    return jnp.stack([sp[:, :, :hd, :hd], sp[:, :, hd:, hd:]], axis=2).reshape(b, 2 * RWKV_PAIRS, hd, hd)


def _prep_weights(p):
    w = {}
    ev = p["ev_w_in"][0]
    gdn_in = 4 * GDN_WIDTH
    w["ev_g"] = ev[:, :gdn_in].astype(BF16)
    w["ev_r"] = ev[:, gdn_in + 2 * GDN_HEADS:].astype(BF16)
    w["ev_bg"] = jnp.pad(ev[:, gdn_in:gdn_in + 2 * GDN_HEADS],
                         ((0, 0), (0, LANES - 2 * GDN_HEADS))).astype(BF16)
    w_out = p["ev_w_out"][0].astype(BF16)
    w["ev_out_a"], w["ev_out_b"] = w_out[:GDN_WIDTH], w_out[GDN_WIDTH:]
    hp = jnp.zeros((SUBLANES, LANES), F32)
    hp = hp.at[0, GDN_HEADS:2 * GDN_HEADS].set(p["gdn_a_log"][0])
    hp = hp.at[1, GDN_HEADS:2 * GDN_HEADS].set(p["gdn_dt_bias"][0])
    w["gdn_hp"] = hp
    names = ("rwkv_w0", "rwkv_a0", "rwkv_k_k", "rwkv_k_a", "rwkv_r_k", "rwkv_gn_g", "rwkv_gn_b")
    rows = [p[n][0] for n in names] + [jnp.zeros((RWKV_WIDTH,), F32)]
    w["rwkv_vecs"] = jnp.stack(rows)
    w2, a2 = p["rwkv_w2"][0], p["rwkv_a2"][0]
    zero = jnp.zeros_like(w2)
    w["rwkv_w2a"] = jnp.concatenate(
        [jnp.concatenate([w2, zero], axis=1), jnp.concatenate([zero, a2], axis=1)], axis=0).astype(BF16)
    w["rwkv_g2"] = p["rwkv_g2"][0].astype(BF16)
    sb = p["sb_w_in"][0].astype(BF16)
    d = sb.shape[0]
    w["sb_q"], w["sb_k"], w["sb_v"] = sb[:, :d], sb[:, d:2 * d], sb[:, 2 * d:]
    w["sb_out"] = p["sb_w_out"][0].astype(BF16)
    for n in ("mem_w_q", "mem_w_k", "mem_w_v", "mem_w_o", "ffn_w_in", "ffn_w_out"):
        w[n] = p[n].astype(BF16)
    return w


def _trunk(x, mem_k, mem_v, mem_stride, conv_buf, s_gdn, s_rwkv, shift, sb_past, p, w):
    b, t, d = x.shape
    m = b * t
    gp, rp, bg = _norm_proj(x.reshape(m, d), p["norm_mix"][0], [w["ev_g"], w["ev_r"], w["ev_bg"]])
    o_a, gdn_new, conv_new = _gdn(gp.reshape(b, t, -1), bg.reshape(b, t, LANES), conv_buf, s_gdn,
                                  p["gdn_conv_w"][0], w["gdn_hp"], p["gdn_norm"][0])
    o_b, rwkv_new, shift_new = _rwkv(rp.reshape(b, t, RWKV_IN), shift, _rwkv_state_to_pairs(s_rwkv),
                                     p["rwkv_mu"][0], w["rwkv_vecs"], w["rwkv_w2a"], w["rwkv_g2"])
    x3 = _mix_mem_attn(x, [o_a, o_b], [w["ev_out_a"], w["ev_out_b"]], p["norm_mem"][0],
                       w["mem_w_q"][0], w["mem_w_o"][0], mem_k, mem_v, 0)
    x4 = _ffn(x3.reshape(m, d), p["norm_ffn"][0], w["ffn_w_in"][0], w["ffn_w_out"][0],
              p["norm_final"], False)
    sb_w = [w["sb_q"], w["sb_k"], w["sb_v"]]
    heads = (b, t, SB_HEADS, SB_HEAD_DIM)
    if sb_past is None:
        outs = [(0, BF16, SB_HEAD_DIM ** -0.5, False), (1, BF16, 1.0, False), (2, BF16, 1.0, False),
                (1, F32, 1.0, True), (2, F32, 1.0, True)]
        q_bf, k_bf, v_bf, k_t, v_t = _norm_proj(x4, p["norm_mix"][1], sb_w, outs, seq_len=t)
        y = _sb_prompt(q_bf.reshape(b, t, d), k_bf.reshape(b, t, d), v_bf.reshape(b, t, d), p["sb_bias"][0])
        k_out, v_out = (jnp.transpose(a.reshape(b, SB_HEADS, SB_HEAD_DIM, t), (0, 3, 1, 2)) for a in (k_t, v_t))
    else:
        q, k, v = (a.reshape(b, t, d) for a in _norm_proj(x4, p["norm_mix"][1], sb_w))
        pool_k, pool_v, page_table = sb_past
        y = _sb_paged(q, k, v, pool_k, pool_v, page_table, 0, p["sb_bias"][0])
        k_out, v_out = k.reshape(heads), v.reshape(heads)
    x6 = _mix_mem_attn(x4.reshape(b, t, d), [y], [w["sb_out"]], p["norm_mem"][1],
                       w["mem_w_q"][1], w["mem_w_o"][1], mem_k, mem_v, mem_stride)
    y_out = _ffn(x6.reshape(m, d), p["norm_ffn"][1], w["ffn_w_in"][1], w["ffn_w_out"][1],
                 p["norm_final"], True).reshape(b, t, d)
    return (y_out, conv_new[None], gdn_new[None], _rwkv_state_from_pairs(rwkv_new)[None],
            shift_new.reshape(b, RWKV_IN)[None], k_out[None], v_out[None])


def kernel(x_prompt, x_sample, mem_prompt, state_gdn, state_gdn_conv, state_rwkv, state_rwkv_shift,
           cache_sb_k, cache_sb_v, cache_mem_k, cache_mem_v, page_table,
           norm_mix, norm_mem, norm_memtok, norm_ffn, norm_final, ev_w_in, ev_w_out,
           gdn_conv_w, gdn_a_log, gdn_dt_bias, gdn_norm, rwkv_mu, rwkv_w0, rwkv_w2, rwkv_a0, rwkv_a2, rwkv_g2,
           rwkv_k_k, rwkv_k_a, rwkv_r_k, rwkv_gn_g, rwkv_gn_b, sb_w_in, sb_w_out, sb_bias,
           mem_w_q, mem_w_k, mem_w_v, mem_w_o, ffn_w_in, ffn_w_out):
    p = dict(norm_mix=norm_mix, norm_mem=norm_mem, norm_memtok=norm_memtok, norm_ffn=norm_ffn,
             norm_final=norm_final, ev_w_in=ev_w_in, ev_w_out=ev_w_out, gdn_conv_w=gdn_conv_w,
             gdn_a_log=gdn_a_log, gdn_dt_bias=gdn_dt_bias, gdn_norm=gdn_norm, rwkv_mu=rwkv_mu,
             rwkv_w0=rwkv_w0, rwkv_w2=rwkv_w2, rwkv_a0=rwkv_a0, rwkv_a2=rwkv_a2, rwkv_g2=rwkv_g2,
             rwkv_k_k=rwkv_k_k, rwkv_k_a=rwkv_k_a, rwkv_r_k=rwkv_r_k, rwkv_gn_g=rwkv_gn_g,
             rwkv_gn_b=rwkv_gn_b, sb_w_in=sb_w_in, sb_w_out=sb_w_out, sb_bias=sb_bias,
             mem_w_q=mem_w_q, mem_w_k=mem_w_k, mem_w_v=mem_w_v, mem_w_o=mem_w_o,
             ffn_w_in=ffn_w_in, ffn_w_out=ffn_w_out)
    assert ev_w_in.shape[0] == 1 and sb_w_in.shape[0] == 1 and norm_mix.shape[0] == 2
    w = _prep_weights(p)
    bp, tp, d = x_prompt.shape
    bs = x_sample.shape[0]
    n_mem = mem_prompt.shape[1]

    mem2d = mem_prompt.reshape(bp * n_mem, d)
    mem_outs = [(0, F32, 1.0, "head_tiles"), (1, F32, 1.0, "head_tiles")]
    hd = d // MEM_HEADS
    tile_rows = n_mem * MEM_HEADS * (hd // LANES)
    mk_l, mv_l, mkt_l, mvt_l = [], [], [], []
    for layer in range(2):
        tiles = _norm_proj(mem2d, norm_memtok[layer], [w["mem_w_k"][layer], w["mem_w_v"][layer]], mem_outs)
        for t2, dst, dst_t in zip(tiles, (mk_l, mv_l), (mkt_l, mvt_l)):
            dst_t.append(t2.reshape(bp, tile_rows, LANES))
            t5 = t2.reshape(bp, n_mem, hd // LANES, MEM_HEADS, LANES)
            dst.append(jnp.transpose(t5, (0, 1, 3, 2, 4)).reshape(bp, n_mem, MEM_HEADS, hd))
    mem_k_p = jnp.stack(mk_l)
    mem_v_p = jnp.stack(mv_l)
    zeros = functools.partial(jnp.zeros, dtype=F32)
    out_p = _trunk(
        x_prompt, jnp.concatenate(mkt_l, axis=0), jnp.concatenate(mvt_l, axis=0), bp,
        zeros((bp, CONV_WIDTH - 1, GDN_CONV_CH)), zeros((bp, GDN_HEADS, GDN_HEAD_DIM, GDN_HEAD_DIM)),
        zeros((bp, 2 * RWKV_PAIRS, RWKV_HEAD_DIM, RWKV_HEAD_DIM)), zeros((bp, RWKV_IN)), None, p, w)

    n_pool = cache_sb_k.shape[1]
    pool_k = jnp.transpose(cache_sb_k, (0, 1, 3, 4, 2)).reshape(cache_sb_k.shape[0] * n_pool, d, PAGE_SIZE)
    pool_v = jnp.transpose(cache_sb_v, (0, 1, 3, 4, 2)).reshape(cache_sb_v.shape[0] * n_pool, d, PAGE_SIZE)
    out_s = _trunk(
        x_sample, _mem_head_tiles(cache_mem_k), _mem_head_tiles(cache_mem_v), bs,
        state_gdn_conv[0], state_gdn[0], state_rwkv[0], state_rwkv_shift[0],
        (pool_k, pool_v, page_table), p, w)

    y_p, conv_p, gdn_p, rwkv_p, shift_p, sbk_p, sbv_p = out_p
    y_s, conv_s, gdn_s, rwkv_s, shift_s, sbk_s, sbv_s = out_s
    return (y_p, y_s, gdn_p, gdn_s, conv_p, conv_s, rwkv_p, rwkv_s, shift_p, shift_s,
            sbk_p, sbk_s, sbv_p, sbv_s, mem_k_p, mem_v_p)
```
